```python
import jax, jax.numpy as jnp
from jax import lax
import numpy as np

D_MODEL = 2048
BATCH = 4
SEQ = 2048
DEPTH = 2
DEC_BATCH = 128
DEC_SEQ = 8
PAST_LEN = 2048
PAGE_SIZE = 128

N_META = 16
N_MIXERS = 2
N_SB_LAYERS = (DEPTH + 1) // N_MIXERS
N_RW_LAYERS = DEPTH // N_MIXERS
RMS_EPS = 1e-6
SB_HEADS = 16
SB_HEAD_DIM = D_MODEL // SB_HEADS
Q_BLOCK = 128
SB_BIAS_INIT = -6.0
RW_HEAD_DIM = 64
RW_HEADS = D_MODEL // RW_HEAD_DIM
W_LORA = max(32, int(round(1.8 * D_MODEL ** 0.5 / 32)) * 32)
A_LORA = max(32, int(round(1.8 * D_MODEL ** 0.5 / 32)) * 32)
G_LORA = max(32, int(round(0.6 * D_MODEL ** 0.8 / 32)) * 32)
N_MU = 6
GN_EPS = 64e-5
N_GROUPS = 4
EXPERTS_PER_GROUP = 8
N_EXPERTS = N_GROUPS * EXPERTS_PER_GROUP
TOP_K = 2
D_EXPERT = D_MODEL // 2
MOE_BLOCK = 128

kernel_name = 'stickbreak_rwkv7_hier_moe_step'


def rms_norm(x, gain):
    xf = x.astype(jnp.float32)
    y = xf * lax.rsqrt(jnp.mean(xf * xf, axis=-1, keepdims=True) + RMS_EPS)
    return (y * gain.astype(jnp.float32)).astype(x.dtype)


def sb_attend(q, k, v, q_pos, k_pos, bias):
    z = jnp.einsum('qhd,khd->hqk', q, k, preferred_element_type=jnp.float32) * (q.shape[-1] ** -0.5)
    z = z + bias.astype(jnp.float32)[:, None, None]
    visible = (k_pos[None, :] < q_pos[:, None])[None]
    log_stay = jnp.where(visible, jax.nn.log_sigmoid(-z), 0.0)
    rev = lax.cumsum(log_stay, axis=2, reverse=True)
    after = jnp.concatenate([rev[..., 1:], jnp.zeros_like(rev[..., :1])], axis=-1)
    w = jnp.where(visible, jnp.exp(jax.nn.log_sigmoid(z) + after), 0.0)
    return jnp.einsum('hqk,khd->qhd', w.astype(v.dtype), v)


def sb_project(xn, w_qkv, g_q, g_k):
    q, k, v = jnp.split(xn @ w_qkv, 3, axis=-1)
    hs = xn.shape[:-1] + (SB_HEADS, SB_HEAD_DIM)
    return rms_norm(q.reshape(hs), g_q), rms_norm(k.reshape(hs), g_k), v.reshape(hs)


def sb_prompt(q, k, v, bias):
    n_b, n_tok = q.shape[:2]
    n_blk = (n_tok - N_META) // Q_BLOCK
    attend = jax.vmap(sb_attend, in_axes=(0, 0, 0, None, None, None))
    meta_pos = jnp.arange(N_META)
    o_meta = attend(q[:, :N_META], k[:, :N_META], v[:, :N_META], meta_pos, meta_pos, bias)
    k_pos = jnp.arange(n_tok)
    q_blk = q[:, N_META:].reshape(n_b, n_blk, Q_BLOCK, SB_HEADS, SB_HEAD_DIM).swapaxes(0, 1)
    starts = N_META + Q_BLOCK * jnp.arange(n_blk)

    def one_block(args):
        qb, s0 = args
        return attend(qb, k, v, s0 + jnp.arange(Q_BLOCK), k_pos, bias)

    o_real = lax.map(one_block, (q_blk, starts)).swapaxes(0, 1)
    o_real = o_real.reshape(n_b, n_tok - N_META, SB_HEADS, SB_HEAD_DIM)
    return jnp.concatenate([o_meta, o_real], axis=1)


def sb_sample(q, k, v, cache_k, cache_v, li, page_table, bias):
    n_new = q.shape[1]
    past_len = page_table.shape[1] * cache_k.shape[2]
    q_pos = past_len + jnp.arange(n_new)
    k_pos = jnp.arange(past_len + n_new)

    def one_seq(args):
        q_s, k_s, v_s, pages = args
        k_past = cache_k[li, pages].reshape(past_len, SB_HEADS, SB_HEAD_DIM)
        v_past = cache_v[li, pages].reshape(past_len, SB_HEADS, SB_HEAD_DIM)
        k_all = jnp.concatenate([k_past, k_s.astype(k_past.dtype)], axis=0)
        v_all = jnp.concatenate([v_past, v_s.astype(v_past.dtype)], axis=0)
        return sb_attend(q_s.astype(k_all.dtype), k_all, v_all, q_pos, k_pos, bias)

    return lax.map(one_seq, (q, k, v, page_table)).astype(q.dtype)


def wkv7_scan(s0, r, dec, k, v, kk, a):
    def step(S, inp):
        r_t, d_t, k_t, v_t, kk_t, a_t = inp
        sa = jnp.einsum('bhvk,bhk->bhv', S, -kk_t)
        S = S * d_t[:, :, None, :] + sa[..., None] * (kk_t * a_t)[:, :, None, :] + v_t[..., None] * k_t[:, :, None, :]
        return S, jnp.einsum('bhvk,bhk->bhv', S, r_t)
    return lax.scan(step, s0, (r, dec, k, v, kk, a))


def rwkv7_mix(xn, shift_prev, s0, mu, w_r, w_k, w_v, w_o, w0, w_la, w_lb, a0, a_la, a_lb, g_la, g_lb, k_k, k_a, r_k, ln_w, ln_b):
    n_b, n_t, d = xn.shape
    f32 = jnp.float32
    x_prev = jnp.concatenate([shift_prev[:, None].astype(xn.dtype), xn[:, :-1]], axis=1)
    xx = x_prev - xn
    x_r, x_w, x_k, x_v, x_a, x_g = (xn + xx * mu[j] for j in range(N_MU))
    r = x_r @ w_r
    k = x_k @ w_k
    v = x_v @ w_v
    log_neg_log_decay = -jax.nn.softplus(-(w0 + jnp.tanh(x_w @ w_la) @ w_lb).astype(f32)) - 0.5
    decay = jnp.exp(-jnp.exp(log_neg_log_decay))
    a = jax.nn.sigmoid((a0 + (x_a @ a_la) @ a_lb).astype(f32))
    g = jax.nn.sigmoid(x_g @ g_la) @ g_lb
    heads = lambda t: t.reshape(n_b, n_t, RW_HEADS, RW_HEAD_DIM)
    kk = heads(k.astype(f32) * k_k.astype(f32))
    kk = kk / jnp.maximum(jnp.sqrt(jnp.sum(kk * kk, axis=-1, keepdims=True)), 1e-12)
    k_h = heads(k.astype(f32) * (1.0 + (a - 1.0) * k_a.astype(f32)))
    r_h, v_h, a_h, d_h = heads(r.astype(f32)), heads(v.astype(f32)), heads(a), heads(decay)
    tm = lambda t: jnp.moveaxis(t, 1, 0)
    s_fin, y = wkv7_scan(s0.astype(f32), tm(r_h), tm(d_h), tm(k_h), tm(v_h), tm(kk), tm(a_h))
    y = jnp.moveaxis(y, 0, 1)
    mean = jnp.mean(y, axis=-1, keepdims=True)
    var = jnp.mean(jnp.square(y - mean), axis=-1, keepdims=True)
    y = ((y - mean) * lax.rsqrt(var + GN_EPS)).reshape(n_b, n_t, d) * ln_w.astype(f32) + ln_b.astype(f32)
    bonus = jnp.sum(r_h * k_h * r_k.astype(f32), axis=-1, keepdims=True) * v_h
    y = y + bonus.reshape(n_b, n_t, d)
    return (y.astype(xn.dtype) * g) @ w_o, xn[:, -1], s_fin


def hier_moe(h, layer, w_group, b_group, w_expert, b_expert, w_up, w_down):
    n_tok, d = h.shape
    f32 = jnp.float32
    g_logit = (h @ w_group[layer]).astype(f32) + b_group[layer].astype(f32)
    g_prob = jax.nn.softmax(g_logit, axis=-1)
    g_idx = jnp.argmax(g_logit, axis=-1)
    g_w = jnp.take_along_axis(g_prob, g_idx[:, None], axis=1)
    e_logit = ((h @ w_expert[layer]).astype(f32) + b_expert[layer].astype(f32)).reshape(n_tok, N_GROUPS, EXPERTS_PER_GROUP)
    e_logit = jnp.take_along_axis(e_logit, g_idx[:, None, None], axis=1)[:, 0]
    top_p, top_i = lax.top_k(jax.nn.softmax(e_logit, axis=-1), TOP_K)
    gates = g_w * top_p / jnp.sum(top_p, axis=-1, keepdims=True)
    flat_e = (g_idx[:, None] * EXPERTS_PER_GROUP + top_i).reshape(-1).astype(jnp.int32)
    n_assign = n_tok * TOP_K
    flat_tok = jnp.repeat(jnp.arange(n_tok, dtype=jnp.int32), TOP_K)
    order = jnp.argsort(flat_e)
    sorted_e = flat_e[order]
    counts = jnp.bincount(flat_e, length=N_EXPERTS)
    starts = jnp.cumsum(counts) - counts
    padded = (counts + MOE_BLOCK - 1) // MOE_BLOCK * MOE_BLOCK
    pad_end = jnp.cumsum(padded)
    pad_start = pad_end - padded
    dest_sorted = pad_start[sorted_e] + jnp.arange(n_assign, dtype=jnp.int32) - starts[sorted_e]
    n_blocks = -(-n_assign // MOE_BLOCK) + N_EXPERTS
    x_disp = jnp.zeros((n_blocks * MOE_BLOCK, d), h.dtype).at[dest_sorted].set(h[flat_tok[order]])
    block_start = jnp.arange(n_blocks, dtype=jnp.int32) * MOE_BLOCK
    block_e = jnp.minimum(jnp.searchsorted(pad_end, block_start, side='right'), N_EXPERTS - 1)

    def expert_block(args):
        xb, e = args
        gate, up = jnp.split(xb @ w_up[layer, e], 2, axis=-1)
        return (jax.nn.silu(gate) * up) @ w_down[layer, e]

    y_disp = lax.map(expert_block, (x_disp.reshape(n_blocks, MOE_BLOCK, d), block_e)).reshape(n_blocks * MOE_BLOCK, d)
    dest = jnp.zeros((n_assign,), jnp.int32).at[order].set(dest_sorted)
    y_assign = y_disp[dest].reshape(n_tok, TOP_K, d)
    return jnp.einsum('tk,tkd->td', gates.astype(h.dtype), y_assign)


def setup_inputs(seed: int = 0) -> dict:
    key = jax.random.key(seed)
    keys = iter(jax.random.split(key, 48))

    def normal(shape, scale=1.0):
        return jax.random.normal(next(keys), shape, jnp.float32) * scale

    def uniform(shape, lo, hi):
        return jax.random.uniform(next(keys), shape, jnp.float32, lo, hi)

    n_pages = PAST_LEN // PAGE_SIZE
    n_pool = (DEC_BATCH * n_pages * 5) // 4
    sb_cache_shape = (N_SB_LAYERS, n_pool, PAGE_SIZE, SB_HEADS, SB_HEAD_DIM)
    page_table = jax.random.permutation(next(keys), n_pool)[: DEC_BATCH * n_pages].reshape(DEC_BATCH, n_pages).astype(jnp.int32)
    d, f = D_MODEL, D_EXPERT
    nr = N_RW_LAYERS
    return {
        'x_prompt': normal((BATCH, SEQ, d)),
        'x_sample': normal((DEC_BATCH, DEC_SEQ, d)),
        'cache_k': normal(sb_cache_shape),
        'cache_v': normal(sb_cache_shape),
        'state_wkv': normal((nr, DEC_BATCH, RW_HEADS, RW_HEAD_DIM, RW_HEAD_DIM), 0.3),
        'state_shift': normal((nr, DEC_BATCH, d)),
        'page_table': page_table,
        'meta_tokens': normal((N_META, d)),
        'ln_mix': 1.0 + normal((DEPTH, d), 0.02),
        'ln_ffn': 1.0 + normal((DEPTH, d), 0.02),
        'sb_w_qkv': normal((N_SB_LAYERS, d, 3 * d), d ** -0.5),
        'sb_q_norm': 1.0 + normal((N_SB_LAYERS, SB_HEAD_DIM), 0.02),
        'sb_k_norm': 1.0 + normal((N_SB_LAYERS, SB_HEAD_DIM), 0.02),
        'sb_logit_bias': SB_BIAS_INIT + normal((N_SB_LAYERS, SB_HEADS), 0.1),
        'sb_w_o': normal((N_SB_LAYERS, d, d), d ** -0.5),
        'rw_mu': uniform((nr, N_MU, d), 0.0, 1.0),
        'rw_w_r': normal((nr, d, d), d ** -0.5),
        'rw_w_k': normal((nr, d, d), d ** -0.5),
        'rw_w_v': normal((nr, d, d), d ** -0.5),
        'rw_w_o': normal((nr, d, d), d ** -0.5),
        'rw_w0': uniform((nr, d), -5.0, 0.5),
        'rw_w_la': normal((nr, d, W_LORA), d ** -0.5),
        'rw_w_lb': normal((nr, W_LORA, d), 0.5 * W_LORA ** -0.5),
        'rw_a0': normal((nr, d), 0.1),
        'rw_a_la': normal((nr, d, A_LORA), d ** -0.5),
        'rw_a_lb': normal((nr, A_LORA, d), 0.5 * A_LORA ** -0.5),
        'rw_g_la': normal((nr, d, G_LORA), d ** -0.5),
        'rw_g_lb': normal((nr, G_LORA, d), G_LORA ** -0.5),
        'rw_k_k': 0.85 + normal((nr, d), 0.02),
        'rw_k_a': 1.0 + normal((nr, d), 0.02),
        'rw_r_k': normal((nr, RW_HEADS, RW_HEAD_DIM), 0.1),
        'rw_ln_w': 1.0 + normal((nr, d), 0.02),
        'rw_ln_b': normal((nr, d), 0.02),
        'moe_w_group': normal((DEPTH, d, N_GROUPS), d ** -0.5),
        'moe_b_group': normal((DEPTH, N_GROUPS), 0.01),
        'moe_w_expert': normal((DEPTH, d, N_EXPERTS), d ** -0.5),
        'moe_b_expert': normal((DEPTH, N_EXPERTS), 0.01),
        'moe_w_up': normal((DEPTH, N_EXPERTS, d, 2 * f), d ** -0.5),
        'moe_w_down': normal((DEPTH, N_EXPERTS, f, d), f ** -0.5),
    }


def reference(x_prompt, x_sample, cache_k, cache_v, state_wkv, state_shift, page_table,
              meta_tokens, ln_mix, ln_ffn, sb_w_qkv, sb_q_norm, sb_k_norm, sb_logit_bias, sb_w_o,
              rw_mu, rw_w_r, rw_w_k, rw_w_v, rw_w_o, rw_w0, rw_w_la, rw_w_lb, rw_a0, rw_a_la, rw_a_lb,
              rw_g_la, rw_g_lb, rw_k_k, rw_k_a, rw_r_k, rw_ln_w, rw_ln_b,
              moe_w_group, moe_b_group, moe_w_expert, moe_b_expert, moe_w_up, moe_w_down):
    n_b = x_prompt.shape[0]
    meta = jnp.broadcast_to(meta_tokens.astype(x_prompt.dtype)[None], (n_b, N_META, D_MODEL))
    hp = jnp.concatenate([meta, x_prompt], axis=1)
    hs = x_sample
    k_p, v_p, k_s, v_s = [], [], [], []
    wkv_p, wkv_s, sh_p, sh_s = [], [], [], []
    for i in range(DEPTH):
        li = i // N_MIXERS
        xp = rms_norm(hp, ln_mix[i])
        xs = rms_norm(hs, ln_mix[i])
        if i % N_MIXERS == 0:
            qp, kp, vp = sb_project(xp, sb_w_qkv[li], sb_q_norm[li], sb_k_norm[li])
            qs, ks, vs = sb_project(xs, sb_w_qkv[li], sb_q_norm[li], sb_k_norm[li])
            op = sb_prompt(qp, kp, vp, sb_logit_bias[li])
            osm = sb_sample(qs, ks, vs, cache_k, cache_v, li, page_table, sb_logit_bias[li])
            hp = hp + op.reshape(hp.shape) @ sb_w_o[li]
            hs = hs + osm.reshape(hs.shape) @ sb_w_o[li]
            k_p.append(kp.astype(cache_k.dtype))
            v_p.append(vp.astype(cache_v.dtype))
            k_s.append(ks.astype(cache_k.dtype))
            v_s.append(vs.astype(cache_v.dtype))
        else:
            rw = (rw_mu[li], rw_w_r[li], rw_w_k[li], rw_w_v[li], rw_w_o[li], rw_w0[li], rw_w_la[li], rw_w_lb[li],
                  rw_a0[li], rw_a_la[li], rw_a_lb[li], rw_g_la[li], rw_g_lb[li], rw_k_k[li], rw_k_a[li], rw_r_k[li],
                  rw_ln_w[li], rw_ln_b[li])
            s0 = jnp.zeros((n_b, RW_HEADS, RW_HEAD_DIM, RW_HEAD_DIM), jnp.float32)
            mp, shp, sp = rwkv7_mix(xp, jnp.zeros_like(xp[:, 0]), s0, *rw)
            ms, shs, ss = rwkv7_mix(xs, state_shift[li], state_wkv[li], *rw)
            hp = hp + mp
            hs = hs + ms
            wkv_p.append(sp.astype(state_wkv.dtype))
            wkv_s.append(ss.astype(state_wkv.dtype))
            sh_p.append(shp.astype(state_shift.dtype))
            sh_s.append(shs.astype(state_shift.dtype))
        fp = rms_norm(hp, ln_ffn[i])
        fs = rms_norm(hs, ln_ffn[i])
        n_p = fp.shape[0] * fp.shape[1]
        tok = jnp.concatenate([fp.reshape(-1, D_MODEL), fs.reshape(-1, D_MODEL)], axis=0)
        ff = hier_moe(tok, i, moe_w_group, moe_b_group, moe_w_expert, moe_b_expert, moe_w_up, moe_w_down)
        hp = hp + ff[:n_p].reshape(hp.shape)
        hs = hs + ff[n_p:].reshape(hs.shape)
    return (hp[:, N_META:], hs, jnp.stack(k_p), jnp.stack(v_p), jnp.stack(k_s), jnp.stack(v_s),
            jnp.stack(wkv_p), jnp.stack(wkv_s), jnp.stack(sh_p), jnp.stack(sh_s))
```

```python
import functools

import jax
import jax.numpy as jnp
from jax import lax
from jax.experimental import pallas as pl
from jax.experimental.pallas import tpu as pltpu

D_MODEL = 2048
N_META = 16
RMS_EPS = 1e-6
SB_HEADS = 16
SB_HEAD_DIM = 128
Q_BLOCK = 128
RW_HEAD_DIM = 64
RW_HEADS = 32
N_MU = 6
GN_EPS = 64e-5
N_GROUPS = 4
EXPERTS_PER_GROUP = 8
N_EXPERTS = 32
TOP_K = 2
MOE_BLOCK = 128

F32 = jnp.float32
BF16 = jnp.bfloat16

VMEM_LIMIT = 56 * 1024 * 1024


def _mm_kernel(x_ref, w_ref, o_ref):
    o_ref[...] = jnp.dot(x_ref[...].astype(BF16), w_ref[...].astype(BF16),
                         preferred_element_type=F32)


def matmul(x, w, tm=512, tn=512):
    m, k = x.shape
    n = w.shape[1]
    tn = min(tn, n)
    tm = min(tm, m)
    return pl.pallas_call(
        _mm_kernel,
        grid=(pl.cdiv(n, tn), pl.cdiv(m, tm)),
        in_specs=[pl.BlockSpec((tm, k), lambda j, i: (i, 0)),
                  pl.BlockSpec((k, tn), lambda j, i: (0, j))],
        out_specs=pl.BlockSpec((tm, tn), lambda j, i: (i, j)),
        out_shape=jax.ShapeDtypeStruct((m, n), F32),
        compiler_params=pltpu.CompilerParams(
            dimension_semantics=("arbitrary", "arbitrary"),
            vmem_limit_bytes=VMEM_LIMIT),
        name="dense_matmul",
    )(x, w)


def mm(x, w):
    lead = x.shape[:-1]
    return matmul(x.reshape(-1, x.shape[-1]), w).reshape(lead + (w.shape[1],))


def rms_norm(x, gain):
    xf = x.astype(F32)
    y = xf * lax.rsqrt(jnp.mean(xf * xf, axis=-1, keepdims=True) + RMS_EPS)
    return (y * gain.astype(F32)).astype(x.dtype)


def sb_attend(q, k, v, q_pos, k_pos, bias):
    z = jnp.einsum('qhd,khd->hqk', q, k, preferred_element_type=F32) * (q.shape[-1] ** -0.5)
    z = z + bias.astype(F32)[:, None, None]
    visible = (k_pos[None, :] < q_pos[:, None])[None]
    log_stay = jnp.where(visible, jax.nn.log_sigmoid(-z), 0.0)
    rev = lax.cumsum(log_stay, axis=2, reverse=True)
    after = jnp.concatenate([rev[..., 1:], jnp.zeros_like(rev[..., :1])], axis=-1)
    w = jnp.where(visible, jnp.exp(jax.nn.log_sigmoid(z) + after), 0.0)
    return jnp.einsum('hqk,khd->qhd', w.astype(v.dtype), v)


def sb_project(xn, w_qkv, g_q, g_k):
    q, k, v = jnp.split(mm(xn, w_qkv), 3, axis=-1)
    hs = xn.shape[:-1] + (SB_HEADS, SB_HEAD_DIM)
    return rms_norm(q.reshape(hs), g_q), rms_norm(k.reshape(hs), g_k), v.reshape(hs)


def sb_prompt(q, k, v, bias):
    n_b, n_tok = q.shape[:2]
    n_blk = (n_tok - N_META) // Q_BLOCK
    attend = jax.vmap(sb_attend, in_axes=(0, 0, 0, None, None, None))
    meta_pos = jnp.arange(N_META)
    o_meta = attend(q[:, :N_META], k[:, :N_META], v[:, :N_META], meta_pos, meta_pos, bias)
    k_pos = jnp.arange(n_tok)
    q_blk = q[:, N_META:].reshape(n_b, n_blk, Q_BLOCK, SB_HEADS, SB_HEAD_DIM).swapaxes(0, 1)
    starts = N_META + Q_BLOCK * jnp.arange(n_blk)

    def one_block(args):
        qb, s0 = args
        return attend(qb, k, v, s0 + jnp.arange(Q_BLOCK), k_pos, bias)

    o_real = lax.map(one_block, (q_blk, starts)).swapaxes(0, 1)
    o_real = o_real.reshape(n_b, n_tok - N_META, SB_HEADS, SB_HEAD_DIM)
    return jnp.concatenate([o_meta, o_real], axis=1)


def sb_sample(q, k, v, cache_k, cache_v, li, page_table, bias):
    n_new = q.shape[1]
    past_len = page_table.shape[1] * cache_k.shape[2]
    q_pos = past_len + jnp.arange(n_new)
    k_pos = jnp.arange(past_len + n_new)

    def one_seq(args):
        q_s, k_s, v_s, pages = args
        k_past = cache_k[li, pages].reshape(past_len, SB_HEADS, SB_HEAD_DIM)
        v_past = cache_v[li, pages].reshape(past_len, SB_HEADS, SB_HEAD_DIM)
        k_all = jnp.concatenate([k_past, k_s.astype(k_past.dtype)], axis=0)
        v_all = jnp.concatenate([v_past, v_s.astype(v_past.dtype)], axis=0)
        return sb_attend(q_s.astype(k_all.dtype), k_all, v_all, q_pos, k_pos, bias)

    return lax.map(one_seq, (q, k, v, page_table)).astype(q.dtype)


def wkv7_scan(s0, r, dec, k, v, kk, a):
    def step(S, inp):
        r_t, d_t, k_t, v_t, kk_t, a_t = inp
        sa = jnp.einsum('bhvk,bhk->bhv', S, -kk_t)
        S = S * d_t[:, :, None, :] + sa[..., None] * (kk_t * a_t)[:, :, None, :] + v_t[..., None] * k_t[:, :, None, :]
        return S, jnp.einsum('bhvk,bhk->bhv', S, r_t)
    return lax.scan(step, s0, (r, dec, k, v, kk, a))


def rwkv7_mix(xn, shift_prev, s0, mu, w_r, w_k, w_v, w_o, w0, w_la, w_lb, a0, a_la, a_lb, g_la, g_lb, k_k, k_a, r_k, ln_w, ln_b):
    n_b, n_t, d = xn.shape
    x_prev = jnp.concatenate([shift_prev[:, None].astype(xn.dtype), xn[:, :-1]], axis=1)
    xx = x_prev - xn
    x_r, x_w, x_k, x_v, x_a, x_g = (xn + xx * mu[j] for j in range(N_MU))
    r = mm(x_r, w_r)
    k = mm(x_k, w_k)
    v = mm(x_v, w_v)
    log_neg_log_decay = -jax.nn.softplus(-(w0 + jnp.tanh(x_w @ w_la) @ w_lb).astype(F32)) - 0.5
    decay = jnp.exp(-jnp.exp(log_neg_log_decay))
    a = jax.nn.sigmoid((a0 + (x_a @ a_la) @ a_lb).astype(F32))
    g = jax.nn.sigmoid(x_g @ g_la) @ g_lb
    heads = lambda t: t.reshape(n_b, n_t, RW_HEADS, RW_HEAD_DIM)
    kk = heads(k.astype(F32) * k_k.astype(F32))
    kk = kk / jnp.maximum(jnp.sqrt(jnp.sum(kk * kk, axis=-1, keepdims=True)), 1e-12)
    k_h = heads(k.astype(F32) * (1.0 + (a - 1.0) * k_a.astype(F32)))
    r_h, v_h, a_h, d_h = heads(r.astype(F32)), heads(v.astype(F32)), heads(a), heads(decay)
    tm = lambda t: jnp.moveaxis(t, 1, 0)
    s_fin, y = wkv7_scan(s0.astype(F32), tm(r_h), tm(d_h), tm(k_h), tm(v_h), tm(kk), tm(a_h))
    y = jnp.moveaxis(y, 0, 1)
    mean = jnp.mean(y, axis=-1, keepdims=True)
    var = jnp.mean(jnp.square(y - mean), axis=-1, keepdims=True)
    y = ((y - mean) * lax.rsqrt(var + GN_EPS)).reshape(n_b, n_t, d) * ln_w.astype(F32) + ln_b.astype(F32)
    bonus = jnp.sum(r_h * k_h * r_k.astype(F32), axis=-1, keepdims=True) * v_h
    y = y + bonus.reshape(n_b, n_t, d)
    return mm(y.astype(xn.dtype) * g, w_o), xn[:, -1], s_fin


def hier_moe(h, layer, w_group, b_group, w_expert, b_expert, w_up, w_down):
    n_tok, d = h.shape
    g_logit = (h @ w_group[layer]).astype(F32) + b_group[layer].astype(F32)
    g_prob = jax.nn.softmax(g_logit, axis=-1)
    g_idx = jnp.argmax(g_logit, axis=-1)
    g_w = jnp.take_along_axis(g_prob, g_idx[:, None], axis=1)
    e_logit = ((h @ w_expert[layer]).astype(F32) + b_expert[layer].astype(F32)).reshape(n_tok, N_GROUPS, EXPERTS_PER_GROUP)
    e_logit = jnp.take_along_axis(e_logit, g_idx[:, None, None], axis=1)[:, 0]
    top_p, top_i = lax.top_k(jax.nn.softmax(e_logit, axis=-1), TOP_K)
    gates = g_w * top_p / jnp.sum(top_p, axis=-1, keepdims=True)
    flat_e = (g_idx[:, None] * EXPERTS_PER_GROUP + top_i).reshape(-1).astype(jnp.int32)
    n_assign = n_tok * TOP_K
    flat_tok = jnp.repeat(jnp.arange(n_tok, dtype=jnp.int32), TOP_K)
    order = jnp.argsort(flat_e)
    sorted_e = flat_e[order]
    counts = jnp.bincount(flat_e, length=N_EXPERTS)
    starts = jnp.cumsum(counts) - counts
    padded = (counts + MOE_BLOCK - 1) // MOE_BLOCK * MOE_BLOCK
    pad_end = jnp.cumsum(padded)
    pad_start = pad_end - padded
    dest_sorted = pad_start[sorted_e] + jnp.arange(n_assign, dtype=jnp.int32) - starts[sorted_e]
    n_blocks = -(-n_assign // MOE_BLOCK) + N_EXPERTS
    x_disp = jnp.zeros((n_blocks * MOE_BLOCK, d), h.dtype).at[dest_sorted].set(h[flat_tok[order]])
    block_start = jnp.arange(n_blocks, dtype=jnp.int32) * MOE_BLOCK
    block_e = jnp.minimum(jnp.searchsorted(pad_end, block_start, side='right'), N_EXPERTS - 1)

    def expert_block(args):
        xb, e = args
        gate, up = jnp.split(xb @ w_up[layer, e], 2, axis=-1)
        return (jax.nn.silu(gate) * up) @ w_down[layer, e]

    y_disp = lax.map(expert_block, (x_disp.reshape(n_blocks, MOE_BLOCK, d), block_e)).reshape(n_blocks * MOE_BLOCK, d)
    dest = jnp.zeros((n_assign,), jnp.int32).at[order].set(dest_sorted)
    y_assign = y_disp[dest].reshape(n_tok, TOP_K, d)
    return jnp.einsum('tk,tkd->td', gates.astype(h.dtype), y_assign)


def kernel(x_prompt, x_sample, cache_k, cache_v, state_wkv, state_shift, page_table, meta_tokens, ln_mix, ln_ffn, sb_w_qkv, sb_q_norm, sb_k_norm, sb_logit_bias, sb_w_o, rw_mu, rw_w_r, rw_w_k, rw_w_v, rw_w_o, rw_w0, rw_w_la, rw_w_lb, rw_a0, rw_a_la, rw_a_lb, rw_g_la, rw_g_lb, rw_k_k, rw_k_a, rw_r_k, rw_ln_w, rw_ln_b, moe_w_group, moe_b_group, moe_w_expert, moe_b_expert, moe_w_up, moe_w_down):
    n_b = x_prompt.shape[0]
    meta = jnp.broadcast_to(meta_tokens.astype(x_prompt.dtype)[None], (n_b, N_META, D_MODEL))
    hp = jnp.concatenate([meta, x_prompt], axis=1)
    hs = x_sample
    k_p, v_p, k_s, v_s = [], [], [], []
    wkv_p, wkv_s, sh_p, sh_s = [], [], [], []
    for i in range(2):
        li = i // 2
        xp = rms_norm(hp, ln_mix[i])
        xs = rms_norm(hs, ln_mix[i])
        if i % 2 == 0:
            qp, kp, vp = sb_project(xp, sb_w_qkv[li], sb_q_norm[li], sb_k_norm[li])
            qs, ks, vs = sb_project(xs, sb_w_qkv[li], sb_q_norm[li], sb_k_norm[li])
            op = sb_prompt(qp, kp, vp, sb_logit_bias[li])
            osm = sb_sample(qs, ks, vs, cache_k, cache_v, li, page_table, sb_logit_bias[li])
            hp = hp + mm(op.reshape(hp.shape), sb_w_o[li])
            hs = hs + mm(osm.reshape(hs.shape), sb_w_o[li])
            k_p.append(kp)
            v_p.append(vp)
            k_s.append(ks)
            v_s.append(vs)
        else:
            rw = (rw_mu[li], rw_w_r[li], rw_w_k[li], rw_w_v[li], rw_w_o[li], rw_w0[li], rw_w_la[li], rw_w_lb[li],
                  rw_a0[li], rw_a_la[li], rw_a_lb[li], rw_g_la[li], rw_g_lb[li], rw_k_k[li], rw_k_a[li], rw_r_k[li],
                  rw_ln_w[li], rw_ln_b[li])
            s0 = jnp.zeros((n_b, RW_HEADS, RW_HEAD_DIM, RW_HEAD_DIM), F32)
            mp, shp, sp = rwkv7_mix(xp, jnp.zeros_like(xp[:, 0]), s0, *rw)
            ms, shs, ss = rwkv7_mix(xs, state_shift[li], state_wkv[li], *rw)
            hp = hp + mp
            hs = hs + ms
            wkv_p.append(sp)
            wkv_s.append(ss)
            sh_p.append(shp)
            sh_s.append(shs)
        fp = rms_norm(hp, ln_ffn[i])
        fs = rms_norm(hs, ln_ffn[i])
        n_p = fp.shape[0] * fp.shape[1]
        tok = jnp.concatenate([fp.reshape(-1, D_MODEL), fs.reshape(-1, D_MODEL)], axis=0)
        ff = hier_moe(tok, i, moe_w_group, moe_b_group, moe_w_expert, moe_b_expert, moe_w_up, moe_w_down)
        hp = hp + ff[:n_p].reshape(hp.shape)
        hs = hs + ff[n_p:].reshape(hs.shape)
    return (hp[:, N_META:], hs, jnp.stack(k_p), jnp.stack(v_p), jnp.stack(k_s), jnp.stack(v_s),
            jnp.stack(wkv_p), jnp.stack(wkv_s), jnp.stack(sh_p), jnp.stack(sh_s))
```

```python
import functools

import jax
import jax.numpy as jnp
from jax import lax
from jax.experimental import pallas as pl
from jax.experimental.pallas import tpu as pltpu

N_META = 16
RMS_EPS = 1e-6
GN_EPS = 64e-5
SB_DH = 128
QB = 128
PAGE = 128
RW_DH = 64
TOP_K = 2
MOE_BLOCK = 128

F32 = jnp.float32
BF16 = jnp.bfloat16
NT_DIMS = (((1,), (1,)), ((), ()))

VMEM_LIMIT = 56 * 1024 * 1024
ROW_TILE = 512
COL_TILE = 512


def _params(n_axes):
    return pltpu.CompilerParams(dimension_semantics=("arbitrary",) * n_axes, vmem_limit_bytes=VMEM_LIMIT)


def _divisor_tile(n, cap, mult):
    best = None
    for t in range(mult, min(n, cap) + 1, mult):
        if n % t == 0:
            best = t
    assert best is not None, (n, cap, mult)
    return best


def _split_dot(x, w2):
    hi = x.astype(BF16)
    lo = (x - hi.astype(F32)).astype(BF16)
    return jnp.dot(jnp.concatenate([hi, lo], axis=1), w2, preferred_element_type=F32)


def _head_sum_weights(dh):
    r = lax.broadcasted_iota(jnp.int32, (256, 128), 0)
    c = lax.broadcasted_iota(jnp.int32, (256, 128), 1)
    return (((r % 128) // dh) == (c // dh)).astype(BF16)


def _head_sum(x, w2):
    return jnp.concatenate([_split_dot(x[:, c:c + 128], w2) for c in range(0, x.shape[1], 128)], axis=1)


def _rmsnorm_kernel(x_ref, g_ref, o_ref):
    x = x_ref[...]
    o_ref[...] = x * lax.rsqrt(jnp.mean(x * x, axis=-1, keepdims=True) + RMS_EPS) * g_ref[...]


def rmsnorm(x, gain):
    m, d = x.shape
    tm = min(ROW_TILE, m)
    return pl.pallas_call(
        _rmsnorm_kernel,
        grid=(pl.cdiv(m, tm),),
        in_specs=[pl.BlockSpec((tm, d), lambda i: (i, 0)), pl.BlockSpec((1, d), lambda i: (0, 0))],
        out_specs=pl.BlockSpec((tm, d), lambda i: (i, 0)),
        out_shape=jax.ShapeDtypeStruct((m, d), F32),
        compiler_params=_params(1),
        name="rmsnorm",
    )(x, gain.reshape(1, d))


def _bdot(x, w_ref):
    return jnp.dot(x.astype(BF16), w_ref[...].astype(BF16), preferred_element_type=F32)


def _mm_kernel(x_ref, w_ref, o_ref):
    o_ref[...] = _bdot(x_ref[...], w_ref)


def _mm_res_kernel(x_ref, w_ref, r_ref, o_ref):
    o_ref[...] = r_ref[...] + _bdot(x_ref[...], w_ref)


def _mm_mix_kernel(x_ref, xp_ref, mu_ref, w_ref, o_ref, *, act):
    x = x_ref[...]
    y = _bdot(x + (xp_ref[...] - x) * mu_ref[...], w_ref)
    if act == "tanh":
        y = jnp.tanh(y)
    elif act == "sigmoid":
        y = jax.nn.sigmoid(y)
    o_ref[...] = y


def _mm_qkv_kernel(x_ref, w_ref, gq_ref, gk_ref, o_ref, *, n_q_tiles):
    j = pl.program_id(0)
    y = _bdot(x_ref[...], w_ref)
    gain = jnp.where(j < n_q_tiles, gq_ref[...], gk_ref[...])
    cols = []
    for c in range(0, y.shape[1], SB_DH):
        yh = y[:, c:c + SB_DH]
        cols.append(yh * lax.rsqrt(jnp.mean(yh * yh, axis=-1, keepdims=True) + RMS_EPS) * gain)
    o_ref[...] = jnp.where(j < 2 * n_q_tiles, jnp.concatenate(cols, axis=1), y)


def _matmul_call(kern, x_like, w, extra_in, extra_specs, name, tn=None):
    m, k = x_like[0].shape
    n = w.shape[1]
    tm, tn = min(ROW_TILE, m), tn or min(COL_TILE, n)
    xs = pl.BlockSpec((tm, k), lambda j, i: (i, 0))
    return pl.pallas_call(
        kern,
        grid=(pl.cdiv(n, tn), pl.cdiv(m, tm)),
        in_specs=[xs] * len(x_like) + extra_specs(tm, tn, k),
        out_specs=pl.BlockSpec((tm, tn), lambda j, i: (i, j)),
        out_shape=jax.ShapeDtypeStruct((m, n), F32),
        compiler_params=_params(2),
        name=name,
    )(*x_like, *extra_in(w))


def _w_spec(tn, k):
    return pl.BlockSpec((k, tn), lambda j, i: (0, j))


def matmul(x, w):
    return _matmul_call(_mm_kernel, [x], w, lambda w: [w], lambda tm, tn, k: [_w_spec(tn, k)], "dense_matmul")


def matmul_residual(x, w, res):
    return _matmul_call(_mm_res_kernel, [x], w, lambda w: [w, res],
                        lambda tm, tn, k: [_w_spec(tn, k), pl.BlockSpec((tm, tn), lambda j, i: (i, j))],
                        "dense_matmul_residual")


def matmul_mix(x, x_prev, mu, w, act=None):
    return _matmul_call(functools.partial(_mm_mix_kernel, act=act), [x, x_prev], w,
                        lambda w: [mu.reshape(1, -1), w],
                        lambda tm, tn, k: [pl.BlockSpec((1, k), lambda j, i: (0, 0)), _w_spec(tn, k)],
                        "token_shift_matmul")


def matmul_qkv(x, w, g_q, g_k):
    d = x.shape[1]
    tn = min(COL_TILE, d)
    assert d % tn == 0 and tn % SB_DH == 0
    gspec = lambda tm, tn_, k: [_w_spec(tn_, k), pl.BlockSpec((1, SB_DH), lambda j, i: (0, 0)),
                                pl.BlockSpec((1, SB_DH), lambda j, i: (0, 0))]
    return _matmul_call(functools.partial(_mm_qkv_kernel, n_q_tiles=d // tn), [x], w,
                        lambda w: [w, g_q.reshape(1, SB_DH), g_k.reshape(1, SB_DH)], gspec, "qkv_headnorm_matmul",
                        tn=tn)


def _suffix_weights():
    r = lax.broadcasted_iota(jnp.int32, (QB, QB), 0)
    c = lax.broadcasted_iota(jnp.int32, (QB, QB), 1)
    w = jnp.concatenate([(r > c).astype(BF16), jnp.ones((QB, QB), BF16)], axis=1)
    return jnp.concatenate([w, w], axis=0)


def _sb_logits(z):
    ls = -(jnp.maximum(z, 0.0) + jnp.log1p(jnp.exp(-jnp.abs(z))))
    return ls, ls + z


def _sb_tile(qbs, kbs, vbs, mask, biases, w2, cbcs, accs):
    lss, lsgs = [], []
    for qb, kb, bias in zip(qbs, kbs, biases):
        z = lax.dot_general(qb, kb, NT_DIMS, preferred_element_type=F32) * (SB_DH ** -0.5) + bias
        ls, lsg = _sb_logits(z)
        lsgs.append(lsg)
        lss.append(ls if mask is None else jnp.where(mask, ls, 0.0))
    res = _split_dot(jnp.concatenate(lss, axis=0), w2)
    new_cbcs, new_accs = [], []
    for i in range(len(qbs)):
        ri = res[QB * i:QB * (i + 1)]
        w = jnp.exp(lsgs[i] + ri[:, :QB] + cbcs[i])
        if mask is not None:
            w = jnp.where(mask, w, 0.0)
        new_accs.append(accs[i] + jnp.dot(w.astype(BF16), vbs[i], preferred_element_type=F32))
        new_cbcs.append(cbcs[i] + ri[:, QB:])
    return tuple(new_cbcs), tuple(new_accs)


def _sbp_kernel(bias_ref, q_ref, k_ref, v_ref, o_ref, kpad, vpad, *, n_blk, nh):
    h0 = pl.program_id(1) * nh
    biases = [bias_ref[h0 + i] for i in range(nh)]
    hs = [slice(SB_DH * i, SB_DH * (i + 1)) for i in range(nh)]
    w2 = _suffix_weights()
    r = lax.broadcasted_iota(jnp.int32, (QB, QB), 0)
    c = lax.broadcasted_iota(jnp.int32, (QB, QB), 1)
    diag_mask = c < r
    meta_mask = c < N_META
    zeros = tuple(jnp.zeros((QB, QB), F32) for _ in range(nh))

    kpad[...] = jnp.zeros_like(kpad)
    vpad[...] = jnp.zeros_like(vpad)
    kpad[0:N_META, :] = k_ref[0:N_META, :].astype(BF16)
    vpad[0:N_META, :] = v_ref[0:N_META, :].astype(BF16)
    rows = lambda ref, r0: [ref[pl.ds(r0, QB), s].astype(BF16) for s in hs]
    kms = [kpad[:, s] for s in hs]
    vms = [vpad[:, s] for s in hs]

    _, accs = _sb_tile(rows(q_ref, 0), kms, vms, diag_mask & meta_mask, biases, w2, zeros, zeros)
    for i in range(nh):
        o_ref[0:N_META, hs[i]] = accs[i][0:N_META]

    def q_block(i, carry):
        q0 = pl.multiple_of(N_META + QB * i, 8)
        qbs = rows(q_ref, q0)
        ca = _sb_tile(qbs, rows(k_ref, q0), rows(v_ref, q0), diag_mask, biases, w2, zeros, zeros)

        def k_block(j, ca):
            k0 = pl.multiple_of(N_META + QB * (i - 1 - j), 8)
            return _sb_tile(qbs, rows(k_ref, k0), rows(v_ref, k0), None, biases, w2, *ca)

        ca = lax.fori_loop(0, i, k_block, ca)
        _, accs = _sb_tile(qbs, kms, vms, meta_mask, biases, w2, *ca)
        for hh in range(nh):
            o_ref[pl.ds(q0, QB), hs[hh]] = accs[hh]
        return carry

    lax.fori_loop(0, n_blk, q_block, 0)


def sb_prompt_attn(qkv, bias, n_b, n_t, nh):
    d = qkv.shape[1] // 3
    hb = d // (SB_DH * nh)
    blk = lambda off: pl.BlockSpec((n_t, SB_DH * nh), lambda b, h, off=off: (b, off + h))
    return pl.pallas_call(
        functools.partial(_sbp_kernel, n_blk=(n_t - N_META) // QB, nh=nh),
        grid=(n_b, hb),
        in_specs=[pl.BlockSpec(memory_space=pltpu.SMEM), blk(0), blk(hb), blk(2 * hb)],
        out_specs=pl.BlockSpec((n_t, SB_DH * nh), lambda b, h: (b, h)),
        out_shape=jax.ShapeDtypeStruct((n_b * n_t, d), F32),
        scratch_shapes=[pltpu.VMEM((QB, SB_DH * nh), BF16), pltpu.VMEM((QB, SB_DH * nh), BF16)],
        compiler_params=_params(2),
        name="sb_prompt_attn",
    )(bias, qkv, qkv, qkv)


def _sbs_kernel(pt_ref, bias_ref, q_ref, kn_ref, vn_ref, kp_ref, vp_ref, o_ref, cbc_scr, acc_scr, pad_scr,
                *, n_heads, n_new):
    j = pl.program_id(1)
    w2 = _suffix_weights()
    hs = [slice(SB_DH * h, SB_DH * (h + 1)) for h in range(n_heads)]
    r = lax.broadcasted_iota(jnp.int32, (n_new * n_heads, PAGE), 0)
    c = lax.broadcasted_iota(jnp.int32, (n_new * n_heads, PAGE), 1)
    bias_col = jnp.concatenate([jnp.full((n_new, 1), bias_ref[h], F32) for h in range(n_heads)], axis=0)

    def page_tile(k_at, v_at, mask, cbc, acc):
        q = q_ref[0]
        z = jnp.concatenate(
            [lax.dot_general(q[n_new * h:n_new * (h + 1)].astype(BF16), k_at(h), NT_DIMS, preferred_element_type=F32)
             for h in range(n_heads)], axis=0) * (SB_DH ** -0.5) + bias_col
        ls, lsg = _sb_logits(z)
        if mask is not None:
            ls = jnp.where(mask, ls, 0.0)
        res = _split_dot(ls, w2)
        w = jnp.exp(lsg + res[:, :PAGE] + cbc)
        if mask is not None:
            w = jnp.where(mask, w, 0.0)
        pv = jnp.concatenate(
            [jnp.dot(w[n_new * h:n_new * (h + 1)].astype(BF16), v_at(h), preferred_element_type=F32)
             for h in range(n_heads)], axis=0)
        return cbc + res[:, PAGE:], acc + pv

    @pl.when(j == 0)
    def _():
        pad_scr[...] = jnp.zeros_like(pad_scr)
        pad_scr[0, 0:n_new, :] = kn_ref[0].astype(BF16)
        pad_scr[1, 0:n_new, :] = vn_ref[0].astype(BF16)
        zeros = jnp.zeros((n_new * n_heads, PAGE), F32)
        cbc, acc = page_tile(lambda h: pad_scr[0, :, hs[h]], lambda h: pad_scr[1, :, hs[h]], c < (r % n_new),
                             zeros, zeros)
        cbc_scr[...] = cbc
        acc_scr[...] = acc

    cbc, acc = page_tile(lambda h: kp_ref[0, 0, :, hs[h]].astype(BF16), lambda h: vp_ref[0, 0, :, hs[h]].astype(BF16),
                         None, cbc_scr[...], acc_scr[...])
    cbc_scr[...] = cbc
    acc_scr[...] = acc

    @pl.when(j == pl.num_programs(1) - 1)
    def _():
        o_ref[0] = acc


def sb_sample_attn(q_hq, k_new, v_new, cache_k, cache_v, layer, page_table, bias):
    n_s, n_pages = page_table.shape
    n_new, d = k_new.shape[1:]
    n_heads = d // SB_DH
    page_map = lambda s, j, pt, b: (layer, pt[s, n_pages - 1 - j], 0, 0)
    seq3 = lambda s, j, pt, b: (s, 0, 0)
    return pl.pallas_call(
        functools.partial(_sbs_kernel, n_heads=n_heads, n_new=n_new),
        grid_spec=pltpu.PrefetchScalarGridSpec(
            num_scalar_prefetch=2,
            grid=(n_s, n_pages),
            in_specs=[pl.BlockSpec((1, n_new * n_heads, SB_DH), seq3),
                      pl.BlockSpec((1, n_new, d), seq3),
                      pl.BlockSpec((1, n_new, d), seq3),
                      pl.BlockSpec((1, 1, PAGE, d), page_map),
                      pl.BlockSpec((1, 1, PAGE, d), page_map)],
            out_specs=pl.BlockSpec((1, n_new * n_heads, SB_DH), seq3),
            scratch_shapes=[pltpu.VMEM((n_new * n_heads, PAGE), F32), pltpu.VMEM((n_new * n_heads, SB_DH), F32),
                            pltpu.VMEM((2, PAGE, d), BF16)]),
        out_shape=jax.ShapeDtypeStruct((n_s, n_new * n_heads, SB_DH), F32),
        compiler_params=_params(2),
        name="sb_sample_attn",
    )(page_table, bias, q_hq, k_new, v_new, cache_k, cache_v)


def _rw_prep_kernel(k_ref, wl_ref, al_ref, w0_ref, a0_ref, kk_ref_, ka_ref, dec_o, kk_o, b_o, km_o):
    k = k_ref[...]
    w2 = _head_sum_weights(RW_DH)
    wpre = w0_ref[...] + wl_ref[...]
    sp = jnp.maximum(-wpre, 0.0) + jnp.log1p(jnp.exp(-jnp.abs(wpre)))
    dec_o[...] = jnp.exp(-jnp.exp(-sp - 0.5))
    a = jax.nn.sigmoid(a0_ref[...] + al_ref[...])
    kk = k * kk_ref_[...]
    kk = kk / jnp.maximum(jnp.sqrt(_head_sum(kk * kk, w2)), 1e-12)
    kk_o[...] = kk
    b_o[...] = kk * a
    km_o[...] = k * (1.0 + (a - 1.0) * ka_ref[...])


def rw_prep(k, wl, al, w0, a0, k_k, k_a):
    m, d = k.shape
    tm = min(256, m)
    tok = pl.BlockSpec((tm, d), lambda i: (i, 0))
    row = pl.BlockSpec((1, d), lambda i: (0, 0))
    out = jax.ShapeDtypeStruct((m, d), F32)
    return pl.pallas_call(
        _rw_prep_kernel,
        grid=(pl.cdiv(m, tm),),
        in_specs=[tok] * 3 + [row] * 4,
        out_specs=[tok] * 4,
        out_shape=[out] * 4,
        compiler_params=_params(1),
        name="rwkv_prep",
    )(k, wl, al, *(p.reshape(1, d) for p in (w0, a0, k_k, k_a)))


def _wkv_kernel(r_ref, d_ref, k_ref, v_ref, kk_ref, b_ref, s0_ref, y_ref, sT_ref, s_scr, *, n_pairs, tc):
    c = pl.program_id(1)

    @pl.when(c == 0)
    def _():
        for p in range(n_pairs):
            s_scr[RW_DH * p:RW_DH * (p + 1), :] = jnp.concatenate([s0_ref[0, 2 * p], s0_ref[0, 2 * p + 1]], axis=-1)

    row = lax.broadcasted_iota(jnp.int32, (RW_DH, 2 * RW_DH), 0)
    col = lax.broadcasted_iota(jnp.int32, (RW_DH, 2 * RW_DH), 1)
    sel = jnp.tile(row == (col % RW_DH), (n_pairs, 1))
    w2 = _head_sum_weights(RW_DH)

    def group(g, carry):
        t0 = pl.multiple_of(g * 8, 8)
        ld = lambda ref: ref[pl.ds(t0, 8), :]
        r8, d8, k8, v8, kk8, b8 = ld(r_ref), ld(d_ref), ld(k_ref), ld(v_ref), ld(kk_ref), ld(b_ref)
        s = s_scr[...]
        ys = []
        for j in range(8):
            def bc(x8):
                return jnp.concatenate(
                    [jnp.broadcast_to(x8[j:j + 1, 2 * RW_DH * p:2 * RW_DH * (p + 1)], (RW_DH, 2 * RW_DH))
                     for p in range(n_pairs)], axis=0)
            sa = _split_dot(s * bc(kk8), w2)
            vbc = _split_dot(jnp.where(sel, bc(v8), 0.0), w2)
            s = s * bc(d8) - sa * bc(b8) + vbc * bc(k8)
            ybc = _split_dot(s * bc(r8), w2)
            ys.append(jnp.sum(jnp.where(sel, ybc, 0.0).reshape(n_pairs, RW_DH, 2 * RW_DH), axis=1))
        s_scr[...] = s
        y_ref[pl.ds(t0, 8), :] = jnp.concatenate(
            [jnp.concatenate([ys[j][p:p + 1] for j in range(8)], axis=0) for p in range(n_pairs)], axis=1)
        return carry

    lax.fori_loop(0, tc // 8, group, 0)

    @pl.when(c == pl.num_programs(1) - 1)
    def _():
        for p in range(n_pairs):
            s = s_scr[RW_DH * p:RW_DH * (p + 1), :]
            sT_ref[0, 2 * p] = s[:, :RW_DH]
            sT_ref[0, 2 * p + 1] = s[:, RW_DH:]


def wkv_scan(r, dec, k, v, kk, b, s0, row0, n_t):
    d = r.shape[1]
    n_b, n_h = s0.shape[:2]
    tc = _divisor_tile(n_t, 48, 8)
    assert row0 % tc == 0
    nc = n_t // tc
    tok = pl.BlockSpec((tc, d), lambda i, c: (row0 // tc + i * nc + c, 0))
    st = pl.BlockSpec((1, n_h, RW_DH, RW_DH), lambda i, c: (i, 0, 0, 0))
    return pl.pallas_call(
        functools.partial(_wkv_kernel, n_pairs=n_h // 2, tc=tc),
        grid=(n_b, nc),
        in_specs=[tok] * 6 + [st],
        out_specs=[pl.BlockSpec((tc, d), lambda i, c: (i * nc + c, 0)), st],
        out_shape=[jax.ShapeDtypeStruct((n_b * n_t, d), F32), jax.ShapeDtypeStruct(s0.shape, F32)],
        scratch_shapes=[pltpu.VMEM((n_h // 2 * RW_DH, 2 * RW_DH), F32)],
        compiler_params=_params(2),
        name="wkv7_scan",
    )(r, dec, k, v, kk, b, s0)


def _rw_post_kernel(y_ref, r_ref, km_ref, v_ref, g_ref, rk_ref, lw_ref, lb_ref, o_ref):
    w2 = _head_sum_weights(RW_DH)
    y = y_ref[...]
    mean = _head_sum(y, w2) * (1.0 / RW_DH)
    yc = y - mean
    var = _head_sum(yc * yc, w2) * (1.0 / RW_DH)
    yn = yc * lax.rsqrt(var + GN_EPS) * lw_ref[...] + lb_ref[...]
    bonus = _head_sum(r_ref[...] * km_ref[...] * rk_ref[...], w2) * v_ref[...]
    o_ref[...] = (yn + bonus) * g_ref[...]


def rw_post(y, r, km, v, g, r_k, ln_w, ln_b):
    m, d = y.shape
    tm = min(256, m)
    tok = pl.BlockSpec((tm, d), lambda i: (i, 0))
    row = pl.BlockSpec((1, d), lambda i: (0, 0))
    return pl.pallas_call(
        _rw_post_kernel,
        grid=(pl.cdiv(m, tm),),
        in_specs=[tok] * 5 + [row] * 3,
        out_specs=tok,
        out_shape=jax.ShapeDtypeStruct((m, d), F32),
        compiler_params=_params(1),
        name="rwkv_post",
    )(y, r, km, v, g, *(p.reshape(1, d) for p in (r_k, ln_w, ln_b)))


def _moe_up_kernel(be_ref, nv_ref, x_ref, wg_ref, wu_ref, o_ref, wg_s, wu_s):
    i = pl.program_id(1)
    e = be_ref[i]
    prev = be_ref[jnp.maximum(i - 1, 0)]

    @pl.when((i == 0) | (e != prev))
    def _():
        wg_s[...] = wg_ref[0, 0].astype(BF16)
        wu_s[...] = wu_ref[0, 0].astype(BF16)

    @pl.when(i < nv_ref[0])
    def _():
        x = x_ref[...].astype(BF16)
        gate = jnp.dot(x, wg_s[...], preferred_element_type=F32)
        up = jnp.dot(x, wu_s[...], preferred_element_type=F32)
        o_ref[...] = gate * jax.nn.sigmoid(gate) * up

    @pl.when(i >= nv_ref[0])
    def _():
        o_ref[...] = jnp.zeros_like(o_ref)


def _moe_down_kernel(be_ref, nv_ref, h_ref, w_ref, o_ref, w_s):
    i = pl.program_id(1)
    e = be_ref[i]
    prev = be_ref[jnp.maximum(i - 1, 0)]

    @pl.when((i == 0) | (e != prev))
    def _():
        w_s[...] = w_ref[0, 0].astype(BF16)

    @pl.when(i < nv_ref[0])
    def _():
        o_ref[...] = jnp.dot(h_ref[...].astype(BF16), w_s[...], preferred_element_type=F32)

    @pl.when(i >= nv_ref[0])
    def _():
        o_ref[...] = jnp.zeros_like(o_ref)


def moe_experts(x_disp, block_e, n_valid, w_up, w_down, layer):
    n_rows, d = x_disp.shape
    n_blocks = n_rows // MOE_BLOCK
    f = w_down.shape[2]
    th = min(COL_TILE, f)
    nf = f // th
    hid = pl.pallas_call(
        _moe_up_kernel,
        grid_spec=pltpu.PrefetchScalarGridSpec(
            num_scalar_prefetch=2,
            grid=(nf, n_blocks),
            in_specs=[pl.BlockSpec((MOE_BLOCK, d), lambda j, i, be, nv: (i, 0)),
                      pl.BlockSpec((1, 1, d, th), lambda j, i, be, nv: (layer, be[i], 0, j)),
                      pl.BlockSpec((1, 1, d, th), lambda j, i, be, nv: (layer, be[i], 0, nf + j))],
            out_specs=pl.BlockSpec((MOE_BLOCK, th), lambda j, i, be, nv: (i, j)),
            scratch_shapes=[pltpu.VMEM((d, th), BF16), pltpu.VMEM((d, th), BF16)]),
        out_shape=jax.ShapeDtypeStruct((n_rows, f), F32),
        compiler_params=_params(2),
        name="moe_up_gate",
    )(block_e, n_valid, x_disp, w_up, w_up)
    tn = min(2 * COL_TILE, d)
    return pl.pallas_call(
        _moe_down_kernel,
        grid_spec=pltpu.PrefetchScalarGridSpec(
            num_scalar_prefetch=2,
            grid=(d // tn, n_blocks),
            in_specs=[pl.BlockSpec((MOE_BLOCK, f), lambda j, i, be, nv: (i, 0)),
                      pl.BlockSpec((1, 1, f, tn), lambda j, i, be, nv: (layer, be[i], 0, j))],
            out_specs=pl.BlockSpec((MOE_BLOCK, tn), lambda j, i, be, nv: (i, j)),
            scratch_shapes=[pltpu.VMEM((f, tn), BF16)]),
        out_shape=jax.ShapeDtypeStruct((n_rows, d), F32),
        compiler_params=_params(2),
        name="moe_down",
    )(block_e, n_valid, hid, w_down)


def hier_moe(h, tok, layer, w_group, b_group, w_expert, b_expert, w_up, w_down):
    n_tok, d = tok.shape
    n_groups = w_group.shape[-1]
    n_experts = w_expert.shape[-1]
    epg = n_experts // n_groups
    logits = matmul(tok, jnp.concatenate([w_group[layer], w_expert[layer]], axis=1))
    g_logit = logits[:, :n_groups] + b_group[layer]
    g_prob = jax.nn.softmax(g_logit, axis=-1)
    g_idx = jnp.argmax(g_logit, axis=-1)
    g_w = jnp.take_along_axis(g_prob, g_idx[:, None], axis=1)
    e_logit = (logits[:, n_groups:] + b_expert[layer]).reshape(n_tok, n_groups, epg)
    e_logit = jnp.take_along_axis(e_logit, g_idx[:, None, None], axis=1)[:, 0]
    top_p, top_i = lax.top_k(jax.nn.softmax(e_logit, axis=-1), TOP_K)
    gates = g_w * top_p / jnp.sum(top_p, axis=-1, keepdims=True)
    flat_e = (g_idx[:, None] * epg + top_i).reshape(-1).astype(jnp.int32)
    n_assign = n_tok * TOP_K
    flat_tok = jnp.repeat(jnp.arange(n_tok, dtype=jnp.int32), TOP_K)
    order = jnp.argsort(flat_e)
    sorted_e = flat_e[order]
    counts = jnp.bincount(flat_e, length=n_experts)
    starts = jnp.cumsum(counts) - counts
    padded = (counts + MOE_BLOCK - 1) // MOE_BLOCK * MOE_BLOCK
    pad_end = jnp.cumsum(padded)
    pad_start = pad_end - padded
    dest_sorted = (pad_start[sorted_e] + jnp.arange(n_assign, dtype=jnp.int32) - starts[sorted_e]).astype(jnp.int32)
    n_blocks = -(-n_assign // MOE_BLOCK) + n_experts
    x_disp = jnp.zeros((n_blocks * MOE_BLOCK, d), tok.dtype).at[dest_sorted].set(tok[flat_tok[order]])
    block_start = jnp.arange(n_blocks, dtype=jnp.int32) * MOE_BLOCK
    block_e = jnp.minimum(jnp.searchsorted(pad_end, block_start, side='right'), n_experts - 1).astype(jnp.int32)
    n_valid = (pad_end[-1:] // MOE_BLOCK).astype(jnp.int32)
    y_disp = moe_experts(x_disp, block_e, n_valid, w_up, w_down, layer)
    dest = jnp.zeros((n_assign,), jnp.int32).at[order].set(dest_sorted)
    y_assign = y_disp[dest].reshape(n_tok, TOP_K, d)
    return h + jnp.sum(gates[:, :, None] * y_assign, axis=1)


def kernel(x_prompt, x_sample, cache_k, cache_v, state_wkv, state_shift, page_table, meta_tokens, ln_mix, ln_ffn, sb_w_qkv, sb_q_norm, sb_k_norm, sb_logit_bias, sb_w_o, rw_mu, rw_w_r, rw_w_k, rw_w_v, rw_w_o, rw_w0, rw_w_la, rw_w_lb, rw_a0, rw_a_la, rw_a_lb, rw_g_la, rw_g_lb, rw_k_k, rw_k_a, rw_r_k, rw_ln_w, rw_ln_b, moe_w_group, moe_b_group, moe_w_expert, moe_b_expert, moe_w_up, moe_w_down):
    n_b, seq, d = x_prompt.shape
    n_s, n_new, _ = x_sample.shape
    n_t = N_META + seq
    n_p = n_b * n_t
    sb_heads = d // SB_DH
    rw_heads = d // RW_DH
    depth = ln_mix.shape[0]
    moe = (moe_w_group, moe_b_group, moe_w_expert, moe_b_expert, moe_w_up, moe_w_down)

    meta = jnp.broadcast_to(meta_tokens[None], (n_b, N_META, d))
    h = jnp.concatenate([jnp.concatenate([meta, x_prompt], axis=1).reshape(n_p, d), x_sample.reshape(n_s * n_new, d)],
                        axis=0)
    pool = cache_k.shape[1]
    cache_k2 = cache_k.reshape(cache_k.shape[0], pool, PAGE, d)
    cache_v2 = cache_v.reshape(cache_v.shape[0], pool, PAGE, d)

    k_p, v_p, k_s, v_s, wkv_p, wkv_s, sh_p, sh_s = [], [], [], [], [], [], [], []
    for i in range(depth):
        li = i // 2
        xn = rmsnorm(h, ln_mix[i])
        if i % 2 == 0:
            qkv = matmul_qkv(xn, sb_w_qkv[li], sb_q_norm[li], sb_k_norm[li])
            kx, vx = qkv[:, d:2 * d], qkv[:, 2 * d:]
            k_p.append(kx[:n_p].reshape(n_b, n_t, sb_heads, SB_DH))
            v_p.append(vx[:n_p].reshape(n_b, n_t, sb_heads, SB_DH))
            k_s.append(kx[n_p:].reshape(n_s, n_new, sb_heads, SB_DH))
            v_s.append(vx[n_p:].reshape(n_s, n_new, sb_heads, SB_DH))
            o_p = sb_prompt_attn(qkv, sb_logit_bias[li], n_b, n_t, nh=min(4, sb_heads))
            q_hq = qkv[n_p:, :d].reshape(n_s, n_new, sb_heads, SB_DH).transpose(0, 2, 1, 3)
            o_s = sb_sample_attn(q_hq.reshape(n_s, sb_heads * n_new, SB_DH), kx[n_p:].reshape(n_s, n_new, d),
                                 vx[n_p:].reshape(n_s, n_new, d), cache_k2, cache_v2, li, page_table,
                                 sb_logit_bias[li])
            o_s = o_s.reshape(n_s, sb_heads, n_new, SB_DH).transpose(0, 2, 1, 3).reshape(n_s * n_new, d)
            h = matmul_residual(jnp.concatenate([o_p, o_s], axis=0), sb_w_o[li], h)
        else:
            xp3 = xn[:n_p].reshape(n_b, n_t, d)
            xs3 = xn[n_p:].reshape(n_s, n_new, d)
            sh_p.append(xp3[:, -1])
            sh_s.append(xs3[:, -1])
            x_prev = jnp.concatenate(
                [jnp.concatenate([jnp.zeros((n_b, 1, d), F32), xp3[:, :-1]], axis=1).reshape(n_p, d),
                 jnp.concatenate([state_shift[li][:, None], xs3[:, :-1]], axis=1).reshape(n_s * n_new, d)], axis=0)
            mu = rw_mu[li]
            r = matmul_mix(xn, x_prev, mu[0], rw_w_r[li])
            k = matmul_mix(xn, x_prev, mu[2], rw_w_k[li])
            v = matmul_mix(xn, x_prev, mu[3], rw_w_v[li])
            wl = matmul(matmul_mix(xn, x_prev, mu[1], rw_w_la[li], act="tanh"), rw_w_lb[li])
            al = matmul(matmul_mix(xn, x_prev, mu[4], rw_a_la[li]), rw_a_lb[li])
            g = matmul(matmul_mix(xn, x_prev, mu[5], rw_g_la[li], act="sigmoid"), rw_g_lb[li])
            dec, kk, b, km = rw_prep(k, wl, al, rw_w0[li], rw_a0[li], rw_k_k[li], rw_k_a[li])
            y_p, s_p = wkv_scan(r, dec, km, v, kk, b, jnp.zeros((n_b, rw_heads, RW_DH, RW_DH), F32), 0, n_t)
            y_s, s_s = wkv_scan(r, dec, km, v, kk, b, state_wkv[li], n_p, n_new)
            wkv_p.append(s_p)
            wkv_s.append(s_s)
            z = rw_post(jnp.concatenate([y_p, y_s], axis=0), r, km, v, g, rw_r_k[li].reshape(-1), rw_ln_w[li],
                        rw_ln_b[li])
            h = matmul_residual(z, rw_w_o[li], h)
        h = hier_moe(h, rmsnorm(h, ln_ffn[i]), i, *moe)
    return (h[:n_p].reshape(n_b, n_t, d)[:, N_META:], h[n_p:].reshape(n_s, n_new, d),
            jnp.stack(k_p), jnp.stack(v_p), jnp.stack(k_s), jnp.stack(v_s),
            jnp.stack(wkv_p), jnp.stack(wkv_s), jnp.stack(sh_p), jnp.stack(sh_s))
```

```python
import functools
import math

import jax
import jax.numpy as jnp
from jax import lax
from jax.experimental import pallas as pl
from jax.experimental.pallas import tpu as pltpu

N_META = 16
RMS_EPS = 1e-6
GN_EPS = 64e-5
SB_DH = 128
QB = 128
PAGE = 128
RW_DH = 64
TOP_K = 2
MOE_BLOCK = 128

F32 = jnp.float32
BF16 = jnp.bfloat16
NT_DIMS = (((1,), (1,)), ((), ()))

VMEM_LIMIT = 56 * 1024 * 1024
ROW_TILE = 512
COL_TILE = 512


def _params(n_axes):
    return pltpu.CompilerParams(dimension_semantics=("arbitrary",) * n_axes, vmem_limit_bytes=VMEM_LIMIT)


def _divisor_tile(n, cap, mult):
    best = None
    for t in range(mult, min(n, cap) + 1, mult):
        if n % t == 0:
            best = t
    assert best is not None, (n, cap, mult)
    return best


def _split_dot(x, w2):
    hi = x.astype(BF16)
    lo = (x - hi.astype(F32)).astype(BF16)
    return jnp.dot(jnp.concatenate([hi, lo], axis=1), w2, preferred_element_type=F32)


def _head_sum_weights(dh):
    r = lax.broadcasted_iota(jnp.int32, (256, 128), 0)
    c = lax.broadcasted_iota(jnp.int32, (256, 128), 1)
    return (((r % 128) // dh) == (c // dh)).astype(BF16)


def _head_sum(x, w2):
    return jnp.concatenate([_split_dot(x[:, c:c + 128], w2) for c in range(0, x.shape[1], 128)], axis=1)


def _rmsnorm_kernel(x_ref, g_ref, o_ref):
    x = x_ref[...]
    o_ref[...] = x * lax.rsqrt(jnp.mean(x * x, axis=-1, keepdims=True) + RMS_EPS) * g_ref[...]


def rmsnorm(x, gain):
    m, d = x.shape
    tm = min(ROW_TILE, m)
    return pl.pallas_call(
        _rmsnorm_kernel,
        grid=(pl.cdiv(m, tm),),
        in_specs=[pl.BlockSpec((tm, d), lambda i: (i, 0)), pl.BlockSpec((1, d), lambda i: (0, 0))],
        out_specs=pl.BlockSpec((tm, d), lambda i: (i, 0)),
        out_shape=jax.ShapeDtypeStruct((m, d), F32),
        compiler_params=_params(1),
        name="rmsnorm",
    )(x, gain.reshape(1, d))


def _bdot(x, w_ref):
    return jnp.dot(x.astype(BF16), w_ref[...].astype(BF16), preferred_element_type=F32)


def _mm_kernel(x_ref, w_ref, o_ref):
    o_ref[...] = _bdot(x_ref[...], w_ref)


def _mm_res_kernel(x_ref, w_ref, r_ref, o_ref):
    o_ref[...] = r_ref[...] + _bdot(x_ref[...], w_ref)


def _mm_mix_kernel(x_ref, xp_ref, mu_ref, w_ref, o_ref, *, act):
    x = x_ref[...]
    y = _bdot(x + (xp_ref[...] - x) * mu_ref[...], w_ref)
    if act == "tanh":
        y = jnp.tanh(y)
    elif act == "sigmoid":
        y = jax.nn.sigmoid(y)
    o_ref[...] = y


def _mm_qkv_kernel(x_ref, w_ref, gq_ref, gk_ref, o_ref, *, n_q_tiles):
    j = pl.program_id(0)
    y = _bdot(x_ref[...], w_ref)
    gain = jnp.where(j < n_q_tiles, gq_ref[...], gk_ref[...])
    cols = []
    for c in range(0, y.shape[1], SB_DH):
        yh = y[:, c:c + SB_DH]
        cols.append(yh * lax.rsqrt(jnp.mean(yh * yh, axis=-1, keepdims=True) + RMS_EPS) * gain)
    o_ref[...] = jnp.where(j < 2 * n_q_tiles, jnp.concatenate(cols, axis=1), y)


def _matmul_call(kern, x_like, w, extra_in, extra_specs, name, tn=None):
    m, k = x_like[0].shape
    n = w.shape[1]
    tm, tn = min(ROW_TILE, m), tn or min(COL_TILE, n)
    xs = pl.BlockSpec((tm, k), lambda j, i: (i, 0))
    return pl.pallas_call(
        kern,
        grid=(pl.cdiv(n, tn), pl.cdiv(m, tm)),
        in_specs=[xs] * len(x_like) + extra_specs(tm, tn, k),
        out_specs=pl.BlockSpec((tm, tn), lambda j, i: (i, j)),
        out_shape=jax.ShapeDtypeStruct((m, n), F32),
        compiler_params=_params(2),
        name=name,
    )(*x_like, *extra_in(w))


def _w_spec(tn, k):
    return pl.BlockSpec((k, tn), lambda j, i: (0, j))


def matmul(x, w):
    return _matmul_call(_mm_kernel, [x], w, lambda w: [w], lambda tm, tn, k: [_w_spec(tn, k)], "dense_matmul")


def matmul_residual(x, w, res):
    return _matmul_call(_mm_res_kernel, [x], w, lambda w: [w, res],
                        lambda tm, tn, k: [_w_spec(tn, k), pl.BlockSpec((tm, tn), lambda j, i: (i, j))],
                        "dense_matmul_residual")


def matmul_mix(x, x_prev, mu, w, act=None):
    return _matmul_call(functools.partial(_mm_mix_kernel, act=act), [x, x_prev], w,
                        lambda w: [mu.reshape(1, -1), w],
                        lambda tm, tn, k: [pl.BlockSpec((1, k), lambda j, i: (0, 0)), _w_spec(tn, k)],
                        "token_shift_matmul")


def matmul_qkv(x, w, g_q, g_k):
    d = x.shape[1]
    tn = min(COL_TILE, d)
    assert d % tn == 0 and tn % SB_DH == 0
    gspec = lambda tm, tn_, k: [_w_spec(tn_, k), pl.BlockSpec((1, SB_DH), lambda j, i: (0, 0)),
                                pl.BlockSpec((1, SB_DH), lambda j, i: (0, 0))]
    return _matmul_call(functools.partial(_mm_qkv_kernel, n_q_tiles=d // tn), [x], w,
                        lambda w: [w, g_q.reshape(1, SB_DH), g_k.reshape(1, SB_DH)], gspec, "qkv_headnorm_matmul",
                        tn=tn)


def _head_rows_kernel(k_ref, v_ref, ko_ref, vo_ref, *, n_heads):
    tm = k_ref.shape[0]
    for h in range(n_heads):
        cols = slice(SB_DH * h, SB_DH * (h + 1))
        ko_ref[pl.ds(h, tm, stride=n_heads), :] = k_ref[:, cols]
        vo_ref[pl.ds(h, tm, stride=n_heads), :] = v_ref[:, cols]


def kv_head_rows(qkv, row0, n_rows):
    d = qkv.shape[1] // 3
    n_heads = d // SB_DH
    tm = _divisor_tile(math.gcd(row0, n_rows), 256, 8)
    out = jax.ShapeDtypeStruct((n_rows * n_heads, SB_DH), F32)
    ospec = pl.BlockSpec((tm * n_heads, SB_DH), lambda i: (i, 0))
    return pl.pallas_call(
        functools.partial(_head_rows_kernel, n_heads=n_heads),
        grid=(n_rows // tm,),
        in_specs=[pl.BlockSpec((tm, d), lambda i: (row0 // tm + i, 1)), pl.BlockSpec((tm, d), lambda i: (row0 // tm + i, 2))],
        out_specs=[ospec, ospec],
        out_shape=[out, out],
        compiler_params=_params(1),
        name="kv_head_rows",
    )(qkv, qkv)


def _suffix_weights():
    r = lax.broadcasted_iota(jnp.int32, (QB, QB), 0)
    c = lax.broadcasted_iota(jnp.int32, (QB, QB), 1)
    w = jnp.concatenate([(r > c).astype(BF16), jnp.ones((QB, QB), BF16)], axis=1)
    return jnp.concatenate([w, w], axis=0)


def _sb_logits(z):
    ls = -(jnp.maximum(z, 0.0) + jnp.log1p(jnp.exp(-jnp.abs(z))))
    return ls, ls + z


def _sb_tile(qbs, kbs, vbs, mask, biases, w2, cbcs, accs):
    lss, lsgs = [], []
    for qb, kb, bias in zip(qbs, kbs, biases):
        z = lax.dot_general(qb, kb, NT_DIMS, preferred_element_type=F32) * (SB_DH ** -0.5) + bias
        ls, lsg = _sb_logits(z)
        lsgs.append(lsg)
        lss.append(ls if mask is None else jnp.where(mask, ls, 0.0))
    res = _split_dot(jnp.concatenate(lss, axis=0), w2)
    new_cbcs, new_accs = [], []
    for i in range(len(qbs)):
        ri = res[QB * i:QB * (i + 1)]
        w = jnp.exp(lsgs[i] + ri[:, :QB] + cbcs[i])
        if mask is not None:
            w = jnp.where(mask, w, 0.0)
        new_accs.append(accs[i] + jnp.dot(w.astype(BF16), vbs[i], preferred_element_type=F32))
        new_cbcs.append(cbcs[i] + ri[:, QB:])
    return tuple(new_cbcs), tuple(new_accs)


def _sbp_kernel(bias_ref, q_ref, k_ref, v_ref, o_ref, kpad, vpad, *, n_blk, nh):
    h0 = pl.program_id(1) * nh
    biases = [bias_ref[h0 + i] for i in range(nh)]
    hs = [slice(SB_DH * i, SB_DH * (i + 1)) for i in range(nh)]
    w2 = _suffix_weights()
    r = lax.broadcasted_iota(jnp.int32, (QB, QB), 0)
    c = lax.broadcasted_iota(jnp.int32, (QB, QB), 1)
    diag_mask = c < r
    meta_mask = c < N_META
    zeros = tuple(jnp.zeros((QB, QB), F32) for _ in range(nh))

    kpad[...] = jnp.zeros_like(kpad)
    vpad[...] = jnp.zeros_like(vpad)
    kpad[0:N_META, :] = k_ref[0:N_META, :].astype(BF16)
    vpad[0:N_META, :] = v_ref[0:N_META, :].astype(BF16)
    rows = lambda ref, r0: [ref[pl.ds(r0, QB), s].astype(BF16) for s in hs]
    kms = [kpad[:, s] for s in hs]
    vms = [vpad[:, s] for s in hs]

    _, accs = _sb_tile(rows(q_ref, 0), kms, vms, diag_mask & meta_mask, biases, w2, zeros, zeros)
    for i in range(nh):
        o_ref[0:N_META, hs[i]] = accs[i][0:N_META]

    def q_block(i, carry):
        q0 = pl.multiple_of(N_META + QB * i, 8)
        qbs = rows(q_ref, q0)
        ca = _sb_tile(qbs, rows(k_ref, q0), rows(v_ref, q0), diag_mask, biases, w2, zeros, zeros)

        def k_block(j, ca):
            k0 = pl.multiple_of(N_META + QB * (i - 1 - j), 8)
            return _sb_tile(qbs, rows(k_ref, k0), rows(v_ref, k0), None, biases, w2, *ca)

        ca = lax.fori_loop(0, i, k_block, ca)
        _, accs = _sb_tile(qbs, kms, vms, meta_mask, biases, w2, *ca)
        for hh in range(nh):
            o_ref[pl.ds(q0, QB), hs[hh]] = accs[hh]
        return carry

    lax.fori_loop(0, n_blk, q_block, 0)


def sb_prompt_attn(qkv, bias, n_b, n_t, nh):
    d = qkv.shape[1] // 3
    hb = d // (SB_DH * nh)
    blk = lambda off: pl.BlockSpec((n_t, SB_DH * nh), lambda b, h, off=off: (b, off + h))
    return pl.pallas_call(
        functools.partial(_sbp_kernel, n_blk=(n_t - N_META) // QB, nh=nh),
        grid=(n_b, hb),
        in_specs=[pl.BlockSpec(memory_space=pltpu.SMEM), blk(0), blk(hb), blk(2 * hb)],
        out_specs=pl.BlockSpec((n_t, SB_DH * nh), lambda b, h: (b, h)),
        out_shape=jax.ShapeDtypeStruct((n_b * n_t, d), F32),
        scratch_shapes=[pltpu.VMEM((QB, SB_DH * nh), BF16), pltpu.VMEM((QB, SB_DH * nh), BF16)],
        compiler_params=_params(2),
        name="sb_prompt_attn",
    )(bias, qkv, qkv, qkv)


def _sbs_kernel(pt_ref, bias_ref, q_ref, kn_ref, vn_ref, *rest, n_heads, n_new, pp):
    kp_refs, vp_refs = rest[:pp], rest[pp:2 * pp]
    o_ref, cbc_scr, acc_scr, pad_scr = rest[2 * pp:]
    j = pl.program_id(1)
    w2 = _suffix_weights()
    hs = [slice(SB_DH * h, SB_DH * (h + 1)) for h in range(n_heads)]
    r = lax.broadcasted_iota(jnp.int32, (n_new * n_heads, PAGE), 0)
    c = lax.broadcasted_iota(jnp.int32, (n_new * n_heads, PAGE), 1)
    bias_col = jnp.concatenate([jnp.full((n_new, 1), bias_ref[h], F32) for h in range(n_heads)], axis=0)
    q = q_ref[0]
    qh = [q[n_new * h:n_new * (h + 1)].astype(BF16) for h in range(n_heads)]
    head_rows = lambda ref, h: ref[0, 0, pl.ds(h, PAGE, stride=n_heads), :].astype(BF16)

    def scores(k_at, mask):
        z = jnp.concatenate([lax.dot_general(qh[h], k_at(h), NT_DIMS, preferred_element_type=F32)
                             for h in range(n_heads)], axis=0) * (SB_DH ** -0.5) + bias_col
        ls, lsg = _sb_logits(z)
        if mask is not None:
            ls = jnp.where(mask, ls, 0.0)
        return lsg, _split_dot(ls, w2)

    def weigh(lsg, res, v_at, mask, cbc, acc):
        w = jnp.exp(lsg + res[:, :PAGE] + cbc)
        if mask is not None:
            w = jnp.where(mask, w, 0.0)
        pv = jnp.concatenate(
            [jnp.dot(w[n_new * h:n_new * (h + 1)].astype(BF16), v_at(h), preferred_element_type=F32)
             for h in range(n_heads)], axis=0)
        return cbc + res[:, PAGE:], acc + pv

    @pl.when(j == 0)
    def _():
        pad_scr[...] = jnp.zeros_like(pad_scr)
        pad_scr[0, 0:n_new, :] = kn_ref[0].astype(BF16)
        pad_scr[1, 0:n_new, :] = vn_ref[0].astype(BF16)
        mask = c < (r % n_new)
        zeros = jnp.zeros((n_new * n_heads, PAGE), F32)
        lsg, res = scores(lambda h: pad_scr[0, :, hs[h]], mask)
        cbc, acc = weigh(lsg, res, lambda h: pad_scr[1, :, hs[h]], mask, zeros, zeros)
        cbc_scr[...] = cbc
        acc_scr[...] = acc

    sc = [scores(functools.partial(head_rows, kp), None) for kp in kp_refs]
    cbc, acc = cbc_scr[...], acc_scr[...]
    for (lsg, res), vp in zip(sc, vp_refs):
        cbc, acc = weigh(lsg, res, functools.partial(head_rows, vp), None, cbc, acc)
    cbc_scr[...] = cbc
    acc_scr[...] = acc

    @pl.when(j == pl.num_programs(1) - 1)
    def _():
        o_ref[0] = acc


def sb_sample_attn(q_hq, k_new, v_new, cache_k, cache_v, layer, page_table, bias, pp):
    n_s, n_pages = page_table.shape
    n_new, d = k_new.shape[1:]
    n_heads = d // SB_DH
    assert n_pages % pp == 0
    page_specs = [pl.BlockSpec((1, 1, PAGE * n_heads, SB_DH),
                               lambda s, j, pt, b, i=i: (layer, pt[s, n_pages - 1 - (j * pp + i)], 0, 0))
                  for i in range(pp)]
    seq3 = lambda s, j, pt, b: (s, 0, 0)
    return pl.pallas_call(
        functools.partial(_sbs_kernel, n_heads=n_heads, n_new=n_new, pp=pp),
        grid_spec=pltpu.PrefetchScalarGridSpec(
            num_scalar_prefetch=2,
            grid=(n_s, n_pages // pp),
            in_specs=[pl.BlockSpec((1, n_new * n_heads, SB_DH), seq3),
                      pl.BlockSpec((1, n_new, d), seq3),
                      pl.BlockSpec((1, n_new, d), seq3)] + page_specs + page_specs,
            out_specs=pl.BlockSpec((1, n_new * n_heads, SB_DH), seq3),
            scratch_shapes=[pltpu.VMEM((n_new * n_heads, PAGE), F32), pltpu.VMEM((n_new * n_heads, SB_DH), F32),
                            pltpu.VMEM((2, PAGE, d), BF16)]),
        out_shape=jax.ShapeDtypeStruct((n_s, n_new * n_heads, SB_DH), F32),
        compiler_params=_params(2),
        name="sb_sample_attn",
    )(page_table, bias, q_hq, k_new, v_new, *([cache_k] * pp), *([cache_v] * pp))


def _rw_prep_kernel(k_ref, wl_ref, al_ref, w0_ref, a0_ref, kk_ref_, ka_ref, dec_o, kk_o, b_o, km_o):
    k = k_ref[...]
    w2 = _head_sum_weights(RW_DH)
    wpre = w0_ref[...] + wl_ref[...]
    sp = jnp.maximum(-wpre, 0.0) + jnp.log1p(jnp.exp(-jnp.abs(wpre)))
    dec_o[...] = jnp.exp(-jnp.exp(-sp - 0.5))
    a = jax.nn.sigmoid(a0_ref[...] + al_ref[...])
    kk = k * kk_ref_[...]
    kk = kk / jnp.maximum(jnp.sqrt(_head_sum(kk * kk, w2)), 1e-12)
    kk_o[...] = kk
    b_o[...] = kk * a
    km_o[...] = k * (1.0 + (a - 1.0) * ka_ref[...])


def rw_prep(k, wl, al, w0, a0, k_k, k_a):
    m, d = k.shape
    tm = min(256, m)
    tok = pl.BlockSpec((tm, d), lambda i: (i, 0))
    row = pl.BlockSpec((1, d), lambda i: (0, 0))
    out = jax.ShapeDtypeStruct((m, d), F32)
    return pl.pallas_call(
        _rw_prep_kernel,
        grid=(pl.cdiv(m, tm),),
        in_specs=[tok] * 3 + [row] * 4,
        out_specs=[tok] * 4,
        out_shape=[out] * 4,
        compiler_params=_params(1),
        name="rwkv_prep",
    )(k, wl, al, *(p.reshape(1, d) for p in (w0, a0, k_k, k_a)))


def _wkv_kernel(r_ref, d_ref, k_ref, v_ref, kk_ref, b_ref, s0_ref, y_ref, sT_ref, s_scr, *, n_pairs, tc):
    c = pl.program_id(1)

    @pl.when(c == 0)
    def _():
        for p in range(n_pairs):
            s_scr[RW_DH * p:RW_DH * (p + 1), :] = jnp.concatenate([s0_ref[0, 2 * p], s0_ref[0, 2 * p + 1]], axis=-1)

    row = lax.broadcasted_iota(jnp.int32, (RW_DH, 2 * RW_DH), 0)
    col = lax.broadcasted_iota(jnp.int32, (RW_DH, 2 * RW_DH), 1)
    sel = jnp.tile(row == (col % RW_DH), (n_pairs, 1))
    w2 = _head_sum_weights(RW_DH)

    def group(g, carry):
        t0 = pl.multiple_of(g * 8, 8)
        ld = lambda ref: ref[pl.ds(t0, 8), :]
        r8, d8, k8, v8, kk8, b8 = ld(r_ref), ld(d_ref), ld(k_ref), ld(v_ref), ld(kk_ref), ld(b_ref)
        s = s_scr[...]
        ys = []
        for j in range(8):
            def bc(x8):
                return jnp.concatenate(
                    [jnp.broadcast_to(x8[j:j + 1, 2 * RW_DH * p:2 * RW_DH * (p + 1)], (RW_DH, 2 * RW_DH))
                     for p in range(n_pairs)], axis=0)
            sa = _split_dot(s * bc(kk8), w2)
            vbc = _split_dot(jnp.where(sel, bc(v8), 0.0), w2)
            s = s * bc(d8) - sa * bc(b8) + vbc * bc(k8)
            ybc = _split_dot(s * bc(r8), w2)
            ys.append(jnp.sum(jnp.where(sel, ybc, 0.0).reshape(n_pairs, RW_DH, 2 * RW_DH), axis=1))
        s_scr[...] = s
        y_ref[pl.ds(t0, 8), :] = jnp.concatenate(
            [jnp.concatenate([ys[j][p:p + 1] for j in range(8)], axis=0) for p in range(n_pairs)], axis=1)
        return carry

    lax.fori_loop(0, tc // 8, group, 0)

    @pl.when(c == pl.num_programs(1) - 1)
    def _():
        for p in range(n_pairs):
            s = s_scr[RW_DH * p:RW_DH * (p + 1), :]
            sT_ref[0, 2 * p] = s[:, :RW_DH]
            sT_ref[0, 2 * p + 1] = s[:, RW_DH:]


def wkv_scan(r, dec, k, v, kk, b, s0, row0, n_t):
    d = r.shape[1]
    n_b, n_h = s0.shape[:2]
    tc = _divisor_tile(n_t, 48, 8)
    assert row0 % tc == 0
    nc = n_t // tc
    tok = pl.BlockSpec((tc, d), lambda i, c: (row0 // tc + i * nc + c, 0))
    st = pl.BlockSpec((1, n_h, RW_DH, RW_DH), lambda i, c: (i, 0, 0, 0))
    return pl.pallas_call(
        functools.partial(_wkv_kernel, n_pairs=n_h // 2, tc=tc),
        grid=(n_b, nc),
        in_specs=[tok] * 6 + [st],
        out_specs=[pl.BlockSpec((tc, d), lambda i, c: (i * nc + c, 0)), st],
        out_shape=[jax.ShapeDtypeStruct((n_b * n_t, d), F32), jax.ShapeDtypeStruct(s0.shape, F32)],
        scratch_shapes=[pltpu.VMEM((n_h // 2 * RW_DH, 2 * RW_DH), F32)],
        compiler_params=_params(2),
        name="wkv7_scan",
    )(r, dec, k, v, kk, b, s0)


def _rw_post_kernel(y_ref, r_ref, km_ref, v_ref, g_ref, rk_ref, lw_ref, lb_ref, o_ref):
    w2 = _head_sum_weights(RW_DH)
    y = y_ref[...]
    mean = _head_sum(y, w2) * (1.0 / RW_DH)
    yc = y - mean
    var = _head_sum(yc * yc, w2) * (1.0 / RW_DH)
    yn = yc * lax.rsqrt(var + GN_EPS) * lw_ref[...] + lb_ref[...]
    bonus = _head_sum(r_ref[...] * km_ref[...] * rk_ref[...], w2) * v_ref[...]
    o_ref[...] = (yn + bonus) * g_ref[...]


def rw_post(y, r, km, v, g, r_k, ln_w, ln_b):
    m, d = y.shape
    tm = min(256, m)
    tok = pl.BlockSpec((tm, d), lambda i: (i, 0))
    row = pl.BlockSpec((1, d), lambda i: (0, 0))
    return pl.pallas_call(
        _rw_post_kernel,
        grid=(pl.cdiv(m, tm),),
        in_specs=[tok] * 5 + [row] * 3,
        out_specs=tok,
        out_shape=jax.ShapeDtypeStruct((m, d), F32),
        compiler_params=_params(1),
        name="rwkv_post",
    )(y, r, km, v, g, *(p.reshape(1, d) for p in (r_k, ln_w, ln_b)))


def _moe_up_kernel(be_ref, nv_ref, x_ref, wg_ref, wu_ref, o_ref, wg_s, wu_s):
    i = pl.program_id(1)
    e = be_ref[i]
    prev = be_ref[jnp.maximum(i - 1, 0)]

    @pl.when((i == 0) | (e != prev))
    def _():
        wg_s[...] = wg_ref[0, 0].astype(BF16)
        wu_s[...] = wu_ref[0, 0].astype(BF16)

    @pl.when(i < nv_ref[0])
    def _():
        x = x_ref[...].astype(BF16)
        gate = jnp.dot(x, wg_s[...], preferred_element_type=F32)
        up = jnp.dot(x, wu_s[...], preferred_element_type=F32)
        o_ref[...] = gate * jax.nn.sigmoid(gate) * up

    @pl.when(i >= nv_ref[0])
    def _():
        o_ref[...] = jnp.zeros_like(o_ref)


def _moe_down_kernel(be_ref, nv_ref, h_ref, w_ref, o_ref, w_s):
    i = pl.program_id(1)
    e = be_ref[i]
    prev = be_ref[jnp.maximum(i - 1, 0)]

    @pl.when((i == 0) | (e != prev))
    def _():
        w_s[...] = w_ref[0, 0].astype(BF16)

    @pl.when(i < nv_ref[0])
    def _():
        o_ref[...] = jnp.dot(h_ref[...].astype(BF16), w_s[...], preferred_element_type=F32)

    @pl.when(i >= nv_ref[0])
    def _():
        o_ref[...] = jnp.zeros_like(o_ref)


def moe_experts(x_disp, block_e, n_valid, w_up, w_down, layer):
    n_rows, d = x_disp.shape
    n_blocks = n_rows // MOE_BLOCK
    f = w_down.shape[2]
    th = min(COL_TILE, f)
    nf = f // th
    hid = pl.pallas_call(
        _moe_up_kernel,
        grid_spec=pltpu.PrefetchScalarGridSpec(
            num_scalar_prefetch=2,
            grid=(nf, n_blocks),
            in_specs=[pl.BlockSpec((MOE_BLOCK, d), lambda j, i, be, nv: (i, 0)),
                      pl.BlockSpec((1, 1, d, th), lambda j, i, be, nv: (layer, be[i], 0, j)),
                      pl.BlockSpec((1, 1, d, th), lambda j, i, be, nv: (layer, be[i], 0, nf + j))],
            out_specs=pl.BlockSpec((MOE_BLOCK, th), lambda j, i, be, nv: (i, j)),
            scratch_shapes=[pltpu.VMEM((d, th), BF16), pltpu.VMEM((d, th), BF16)]),
        out_shape=jax.ShapeDtypeStruct((n_rows, f), F32),
        compiler_params=_params(2),
        name="moe_up_gate",
    )(block_e, n_valid, x_disp, w_up, w_up)
    tn = min(2 * COL_TILE, d)
    return pl.pallas_call(
        _moe_down_kernel,
        grid_spec=pltpu.PrefetchScalarGridSpec(
            num_scalar_prefetch=2,
            grid=(d // tn, n_blocks),
            in_specs=[pl.BlockSpec((MOE_BLOCK, f), lambda j, i, be, nv: (i, 0)),
                      pl.BlockSpec((1, 1, f, tn), lambda j, i, be, nv: (layer, be[i], 0, j))],
            out_specs=pl.BlockSpec((MOE_BLOCK, tn), lambda j, i, be, nv: (i, j)),
            scratch_shapes=[pltpu.VMEM((f, tn), BF16)]),
        out_shape=jax.ShapeDtypeStruct((n_rows, d), F32),
        compiler_params=_params(2),
        name="moe_down",
    )(block_e, n_valid, hid, w_down)


def hier_moe(h, tok, layer, w_group, b_group, w_expert, b_expert, w_up, w_down):
    n_tok, d = tok.shape
    n_groups = w_group.shape[-1]
    n_experts = w_expert.shape[-1]
    epg = n_experts // n_groups
    logits = matmul(tok, jnp.concatenate([w_group[layer], w_expert[layer]], axis=1))
    g_logit = logits[:, :n_groups] + b_group[layer]
    g_prob = jax.nn.softmax(g_logit, axis=-1)
    g_idx = jnp.argmax(g_logit, axis=-1)
    g_w = jnp.take_along_axis(g_prob, g_idx[:, None], axis=1)
    e_logit = (logits[:, n_groups:] + b_expert[layer]).reshape(n_tok, n_groups, epg)
    e_logit = jnp.take_along_axis(e_logit, g_idx[:, None, None], axis=1)[:, 0]
    top_p, top_i = lax.top_k(jax.nn.softmax(e_logit, axis=-1), TOP_K)
    gates = g_w * top_p / jnp.sum(top_p, axis=-1, keepdims=True)
    flat_e = (g_idx[:, None] * epg + top_i).reshape(-1).astype(jnp.int32)
    n_assign = n_tok * TOP_K
    flat_tok = jnp.repeat(jnp.arange(n_tok, dtype=jnp.int32), TOP_K)
    order = jnp.argsort(flat_e)
    sorted_e = flat_e[order]
    counts = jnp.sum(flat_e[:, None] == jnp.arange(n_experts, dtype=jnp.int32)[None, :], axis=0, dtype=jnp.int32)
    starts = jnp.cumsum(counts) - counts
    padded = (counts + MOE_BLOCK - 1) // MOE_BLOCK * MOE_BLOCK
    pad_end = jnp.cumsum(padded)
    pad_start = pad_end - padded
    dest_sorted = (pad_start[sorted_e] + jnp.arange(n_assign, dtype=jnp.int32) - starts[sorted_e]).astype(jnp.int32)
    n_blocks = -(-n_assign // MOE_BLOCK) + n_experts
    src_tok = jnp.zeros((n_blocks * MOE_BLOCK,), jnp.int32).at[dest_sorted].set(flat_tok[order])
    x_disp = tok[src_tok]
    block_start = jnp.arange(n_blocks, dtype=jnp.int32) * MOE_BLOCK
    block_e = jnp.minimum(jnp.sum(pad_end[None, :] <= block_start[:, None], axis=1), n_experts - 1).astype(jnp.int32)
    n_valid = (pad_end[-1:] // MOE_BLOCK).astype(jnp.int32)
    y_disp = moe_experts(x_disp, block_e, n_valid, w_up, w_down, layer)
    dest = jnp.zeros((n_assign,), jnp.int32).at[order].set(dest_sorted).reshape(n_tok, TOP_K)
    for kth in range(TOP_K):
        h = h + gates[:, kth:kth + 1] * y_disp[dest[:, kth]]
    return h


def kernel(x_prompt, x_sample, cache_k, cache_v, state_wkv, state_shift, page_table, meta_tokens, ln_mix, ln_ffn, sb_w_qkv, sb_q_norm, sb_k_norm, sb_logit_bias, sb_w_o, rw_mu, rw_w_r, rw_w_k, rw_w_v, rw_w_o, rw_w0, rw_w_la, rw_w_lb, rw_a0, rw_a_la, rw_a_lb, rw_g_la, rw_g_lb, rw_k_k, rw_k_a, rw_r_k, rw_ln_w, rw_ln_b, moe_w_group, moe_b_group, moe_w_expert, moe_b_expert, moe_w_up, moe_w_down):
    n_b, seq, d = x_prompt.shape
    n_s, n_new, _ = x_sample.shape
    n_t = N_META + seq
    n_p = n_b * n_t
    sb_heads = d // SB_DH
    rw_heads = d // RW_DH
    depth = ln_mix.shape[0]
    moe = (moe_w_group, moe_b_group, moe_w_expert, moe_b_expert, moe_w_up, moe_w_down)

    meta = jnp.broadcast_to(meta_tokens[None], (n_b, N_META, d))
    h = jnp.concatenate([jnp.concatenate([meta, x_prompt], axis=1).reshape(n_p, d), x_sample.reshape(n_s * n_new, d)],
                        axis=0)
    pool = cache_k.shape[1]
    cache_k2 = cache_k.reshape(cache_k.shape[0], pool, PAGE * sb_heads, SB_DH)
    cache_v2 = cache_v.reshape(cache_v.shape[0], pool, PAGE * sb_heads, SB_DH)

    k_p, v_p, k_s, v_s, wkv_p, wkv_s, sh_p, sh_s = [], [], [], [], [], [], [], []
    for i in range(depth):
        li = i // 2
        xn = rmsnorm(h, ln_mix[i])
        if i % 2 == 0:
            qkv = matmul_qkv(xn, sb_w_qkv[li], sb_q_norm[li], sb_k_norm[li])
            kx, vx = qkv[n_p:, d:2 * d], qkv[n_p:, 2 * d:]
            k3, v3 = kv_head_rows(qkv, 0, n_p)
            k_p.append(k3.reshape(n_b, n_t, sb_heads, SB_DH))
            v_p.append(v3.reshape(n_b, n_t, sb_heads, SB_DH))
            k3, v3 = kv_head_rows(qkv, n_p, n_s * n_new)
            k_s.append(k3.reshape(n_s, n_new, sb_heads, SB_DH))
            v_s.append(v3.reshape(n_s, n_new, sb_heads, SB_DH))
            o_p = sb_prompt_attn(qkv, sb_logit_bias[li], n_b, n_t, nh=min(4, sb_heads))
            q_hq = qkv[n_p:, :d].reshape(n_s, n_new, sb_heads, SB_DH).transpose(0, 2, 1, 3)
            o_s = sb_sample_attn(q_hq.reshape(n_s, sb_heads * n_new, SB_DH), kx.reshape(n_s, n_new, d),
                                 vx.reshape(n_s, n_new, d), cache_k2, cache_v2, li, page_table,
                                 sb_logit_bias[li], pp=2 if page_table.shape[1] % 2 == 0 else 1)
            o_s = o_s.reshape(n_s, sb_heads, n_new, SB_DH).transpose(0, 2, 1, 3).reshape(n_s * n_new, d)
            h = matmul_residual(jnp.concatenate([o_p, o_s], axis=0), sb_w_o[li], h)
        else:
            xp3 = xn[:n_p].reshape(n_b, n_t, d)
            xs3 = xn[n_p:].reshape(n_s, n_new, d)
            sh_p.append(xp3[:, -1])
            sh_s.append(xs3[:, -1])
            x_prev = jnp.concatenate(
                [jnp.concatenate([jnp.zeros((n_b, 1, d), F32), xp3[:, :-1]], axis=1).reshape(n_p, d),
                 jnp.concatenate([state_shift[li][:, None], xs3[:, :-1]], axis=1).reshape(n_s * n_new, d)], axis=0)
            mu = rw_mu[li]
            r = matmul_mix(xn, x_prev, mu[0], rw_w_r[li])
            k = matmul_mix(xn, x_prev, mu[2], rw_w_k[li])
            v = matmul_mix(xn, x_prev, mu[3], rw_w_v[li])
            wl = matmul(matmul_mix(xn, x_prev, mu[1], rw_w_la[li], act="tanh"), rw_w_lb[li])
            al = matmul(matmul_mix(xn, x_prev, mu[4], rw_a_la[li]), rw_a_lb[li])
            g = matmul(matmul_mix(xn, x_prev, mu[5], rw_g_la[li], act="sigmoid"), rw_g_lb[li])
            dec, kk, b, km = rw_prep(k, wl, al, rw_w0[li], rw_a0[li], rw_k_k[li], rw_k_a[li])
            y_p, s_p = wkv_scan(r, dec, km, v, kk, b, jnp.zeros((n_b, rw_heads, RW_DH, RW_DH), F32), 0, n_t)
            y_s, s_s = wkv_scan(r, dec, km, v, kk, b, state_wkv[li], n_p, n_new)
            wkv_p.append(s_p)
            wkv_s.append(s_s)
            z = rw_post(jnp.concatenate([y_p, y_s], axis=0), r, km, v, g, rw_r_k[li].reshape(-1), rw_ln_w[li],
                        rw_ln_b[li])
            h = matmul_residual(z, rw_w_o[li], h)
        h = hier_moe(h, rmsnorm(h, ln_ffn[i]), i, *moe)
    return (h[:n_p].reshape(n_b, n_t, d)[:, N_META:], h[n_p:].reshape(n_s, n_new, d),
            jnp.stack(k_p), jnp.stack(v_p), jnp.stack(k_s), jnp.stack(v_s),
            jnp.stack(wkv_p), jnp.stack(wkv_s), jnp.stack(sh_p), jnp.stack(sh_s))
```

```python
import functools
import math

import jax
import jax.numpy as jnp
from jax import lax
from jax.experimental import pallas as pl
from jax.experimental.pallas import tpu as pltpu

N_META = 16
RMS_EPS = 1e-6
GN_EPS = 64e-5
SB_DH = 128
QB = 128
PAGE = 128
RW_DH = 64
TOP_K = 2
MOE_BLOCK = 128

F32 = jnp.float32
BF16 = jnp.bfloat16
NT_DIMS = (((1,), (1,)), ((), ()))

VMEM_LIMIT = 56 * 1024 * 1024
ROW_TILE = 512
COL_TILE = 1024


def _params(n_axes):
    return pltpu.CompilerParams(dimension_semantics=("arbitrary",) * n_axes, vmem_limit_bytes=VMEM_LIMIT)


def _divisor_tile(n, cap, mult):
    best = None
    for t in range(mult, min(n, cap) + 1, mult):
        if n % t == 0:
            best = t
    assert best is not None, (n, cap, mult)
    return best


def _split_dot(x, w2):
    hi = x.astype(BF16)
    lo = (x - hi.astype(F32)).astype(BF16)
    return jnp.dot(jnp.concatenate([hi, lo], axis=1), w2, preferred_element_type=F32)


def _head_sum_weights(dh):
    r = lax.broadcasted_iota(jnp.int32, (256, 128), 0)
    c = lax.broadcasted_iota(jnp.int32, (256, 128), 1)
    return (((r % 128) // dh) == (c // dh)).astype(BF16)


def _head_sum(x, w2):
    return jnp.concatenate([_split_dot(x[:, c:c + 128], w2) for c in range(0, x.shape[1], 128)], axis=1)


def _rmsnorm_kernel(x_ref, g_ref, o_ref):
    x = x_ref[...]
    o_ref[...] = x * lax.rsqrt(jnp.mean(x * x, axis=-1, keepdims=True) + RMS_EPS) * g_ref[...]


def rmsnorm(x, gain):
    m, d = x.shape
    tm = min(ROW_TILE, m)
    return pl.pallas_call(
        _rmsnorm_kernel,
        grid=(pl.cdiv(m, tm),),
        in_specs=[pl.BlockSpec((tm, d), lambda i: (i, 0)), pl.BlockSpec((1, d), lambda i: (0, 0))],
        out_specs=pl.BlockSpec((tm, d), lambda i: (i, 0)),
        out_shape=jax.ShapeDtypeStruct((m, d), F32),
        compiler_params=_params(1),
        name="rmsnorm",
    )(x, gain.reshape(1, d))


def _bdot(x, w_ref, w_s):
    @pl.when(pl.program_id(1) == 0)
    def _():
        w_s[...] = w_ref[...].astype(BF16)
    return jnp.dot(x.astype(BF16), w_s[...], preferred_element_type=F32)


def _mm_kernel(x_ref, w_ref, o_ref, w_s):
    o_ref[...] = _bdot(x_ref[...], w_ref, w_s)


def _mm_res_kernel(x_ref, w_ref, r_ref, o_ref, w_s):
    o_ref[...] = r_ref[...] + _bdot(x_ref[...], w_ref, w_s)


def _mm_mix_kernel(x_ref, xp_ref, mu_ref, w_ref, o_ref, w_s, *, act):
    x = x_ref[...]
    y = _bdot(x + (xp_ref[...] - x) * mu_ref[...], w_ref, w_s)
    if act == "tanh":
        y = jnp.tanh(y)
    elif act == "sigmoid":
        y = jax.nn.sigmoid(y)
    o_ref[...] = y


def _mm_qkv_kernel(x_ref, w_ref, gq_ref, gk_ref, o_ref, w_s, *, n_q_tiles):
    j = pl.program_id(0)
    y = _bdot(x_ref[...], w_ref, w_s)
    gain = jnp.where(j < n_q_tiles, gq_ref[...], gk_ref[...])
    cols = []
    for c in range(0, y.shape[1], SB_DH):
        yh = y[:, c:c + SB_DH]
        cols.append(yh * lax.rsqrt(jnp.mean(yh * yh, axis=-1, keepdims=True) + RMS_EPS) * gain)
    o_ref[...] = jnp.where(j < 2 * n_q_tiles, jnp.concatenate(cols, axis=1), y)


def _matmul_call(kern, x_like, w, extra_in, extra_specs, name, tn=None):
    m, k = x_like[0].shape
    n = w.shape[1]
    tm, tn = min(ROW_TILE, m), tn or min(COL_TILE, n)
    xs = pl.BlockSpec((tm, k), lambda j, i: (i, 0))
    return pl.pallas_call(
        kern,
        grid=(pl.cdiv(n, tn), pl.cdiv(m, tm)),
        in_specs=[xs] * len(x_like) + extra_specs(tm, tn, k),
        out_specs=pl.BlockSpec((tm, tn), lambda j, i: (i, j)),
        out_shape=jax.ShapeDtypeStruct((m, n), F32),
        scratch_shapes=[pltpu.VMEM((k, tn), BF16)],
        compiler_params=_params(2),
        name=name,
    )(*x_like, *extra_in(w))


def _w_spec(tn, k):
    return pl.BlockSpec((k, tn), lambda j, i: (0, j))


def matmul(x, w):
    return _matmul_call(_mm_kernel, [x], w, lambda w: [w], lambda tm, tn, k: [_w_spec(tn, k)], "dense_matmul")


def matmul_residual(x, w, res):
    return _matmul_call(_mm_res_kernel, [x], w, lambda w: [w, res],
                        lambda tm, tn, k: [_w_spec(tn, k), pl.BlockSpec((tm, tn), lambda j, i: (i, j))],
                        "dense_matmul_residual")


def matmul_mix(x, x_prev, mu, w, act=None):
    return _matmul_call(functools.partial(_mm_mix_kernel, act=act), [x, x_prev], w,
                        lambda w: [mu.reshape(1, -1), w],
                        lambda tm, tn, k: [pl.BlockSpec((1, k), lambda j, i: (0, 0)), _w_spec(tn, k)],
                        "token_shift_matmul")


def matmul_qkv(x, w, g_q, g_k):
    d = x.shape[1]
    tn = min(COL_TILE, d)
    assert d % tn == 0 and tn % SB_DH == 0
    gspec = lambda tm, tn_, k: [_w_spec(tn_, k), pl.BlockSpec((1, SB_DH), lambda j, i: (0, 0)),
                                pl.BlockSpec((1, SB_DH), lambda j, i: (0, 0))]
    return _matmul_call(functools.partial(_mm_qkv_kernel, n_q_tiles=d // tn), [x], w,
                        lambda w: [w, g_q.reshape(1, SB_DH), g_k.reshape(1, SB_DH)], gspec, "qkv_headnorm_matmul",
                        tn=tn)


def _head_rows_kernel(k_ref, v_ref, ko_ref, vo_ref, *, n_heads):
    tm = k_ref.shape[0]
    for h in range(n_heads):
        cols = slice(SB_DH * h, SB_DH * (h + 1))
        ko_ref[pl.ds(h, tm, stride=n_heads), :] = k_ref[:, cols]
        vo_ref[pl.ds(h, tm, stride=n_heads), :] = v_ref[:, cols]


def kv_head_rows(qkv, row0, n_rows):
    d = qkv.shape[1] // 3
    n_heads = d // SB_DH
    tm = _divisor_tile(math.gcd(row0, n_rows), 256, 8)
    out = jax.ShapeDtypeStruct((n_rows * n_heads, SB_DH), F32)
    ospec = pl.BlockSpec((tm * n_heads, SB_DH), lambda i: (i, 0))
    return pl.pallas_call(
        functools.partial(_head_rows_kernel, n_heads=n_heads),
        grid=(n_rows // tm,),
        in_specs=[pl.BlockSpec((tm, d), lambda i: (row0 // tm + i, 1)), pl.BlockSpec((tm, d), lambda i: (row0 // tm + i, 2))],
        out_specs=[ospec, ospec],
        out_shape=[out, out],
        compiler_params=_params(1),
        name="kv_head_rows",
    )(qkv, qkv)


def _suffix_weights():
    r = lax.broadcasted_iota(jnp.int32, (QB, QB), 0)
    c = lax.broadcasted_iota(jnp.int32, (QB, QB), 1)
    w = jnp.concatenate([(r > c).astype(BF16), jnp.ones((QB, QB), BF16)], axis=1)
    return jnp.concatenate([w, w], axis=0)


def _sb_logits(z):
    ls = -(jnp.maximum(z, 0.0) + jnp.log1p(jnp.exp(-jnp.abs(z))))
    return ls, ls + z


def _sb_tile(qbs, kbs, vbs, mask, biases, w2, cbcs, accs):
    lss, lsgs = [], []
    for qb, kb, bias in zip(qbs, kbs, biases):
        z = lax.dot_general(qb, kb, NT_DIMS, preferred_element_type=F32) * (SB_DH ** -0.5) + bias
        ls, lsg = _sb_logits(z)
        lsgs.append(lsg)
        lss.append(ls if mask is None else jnp.where(mask, ls, 0.0))
    res = _split_dot(jnp.concatenate(lss, axis=0), w2)
    new_cbcs, new_accs = [], []
    for i in range(len(qbs)):
        ri = res[QB * i:QB * (i + 1)]
        w = jnp.exp(lsgs[i] + ri[:, :QB] + cbcs[i])
        if mask is not None:
            w = jnp.where(mask, w, 0.0)
        new_accs.append(accs[i] + jnp.dot(w.astype(BF16), vbs[i], preferred_element_type=F32))
        new_cbcs.append(cbcs[i] + ri[:, QB:])
    return tuple(new_cbcs), tuple(new_accs)


def _sbp_kernel(bias_ref, q_ref, k_ref, v_ref, o_ref, kpad, vpad, *, n_blk, nh):
    h0 = pl.program_id(1) * nh
    biases = [bias_ref[h0 + i] for i in range(nh)]
    hs = [slice(SB_DH * i, SB_DH * (i + 1)) for i in range(nh)]
    w2 = _suffix_weights()
    r = lax.broadcasted_iota(jnp.int32, (QB, QB), 0)
    c = lax.broadcasted_iota(jnp.int32, (QB, QB), 1)
    diag_mask = c < r
    meta_mask = c < N_META
    zeros = tuple(jnp.zeros((QB, QB), F32) for _ in range(nh))

    kpad[...] = jnp.zeros_like(kpad)
    vpad[...] = jnp.zeros_like(vpad)
    kpad[0:N_META, :] = k_ref[0:N_META, :].astype(BF16)
    vpad[0:N_META, :] = v_ref[0:N_META, :].astype(BF16)
    rows = lambda ref, r0: [ref[pl.ds(r0, QB), s].astype(BF16) for s in hs]
    kms = [kpad[:, s] for s in hs]
    vms = [vpad[:, s] for s in hs]

    _, accs = _sb_tile(rows(q_ref, 0), kms, vms, diag_mask & meta_mask, biases, w2, zeros, zeros)
    for i in range(nh):
        o_ref[0:N_META, hs[i]] = accs[i][0:N_META]

    def q_block(i, carry):
        q0 = pl.multiple_of(N_META + QB * i, 8)
        qbs = rows(q_ref, q0)
        ca = _sb_tile(qbs, rows(k_ref, q0), rows(v_ref, q0), diag_mask, biases, w2, zeros, zeros)

        def k_block(j, ca):
            k0 = pl.multiple_of(N_META + QB * (i - 1 - j), 8)
            return _sb_tile(qbs, rows(k_ref, k0), rows(v_ref, k0), None, biases, w2, *ca)

        ca = lax.fori_loop(0, i, k_block, ca)
        _, accs = _sb_tile(qbs, kms, vms, meta_mask, biases, w2, *ca)
        for hh in range(nh):
            o_ref[pl.ds(q0, QB), hs[hh]] = accs[hh]
        return carry

    lax.fori_loop(0, n_blk, q_block, 0)


def sb_prompt_attn(qkv, bias, n_b, n_t, nh):
    d = qkv.shape[1] // 3
    hb = d // (SB_DH * nh)
    blk = lambda off: pl.BlockSpec((n_t, SB_DH * nh), lambda b, h, off=off: (b, off + h))
    return pl.pallas_call(
        functools.partial(_sbp_kernel, n_blk=(n_t - N_META) // QB, nh=nh),
        grid=(n_b, hb),
        in_specs=[pl.BlockSpec(memory_space=pltpu.SMEM), blk(0), blk(hb), blk(2 * hb)],
        out_specs=pl.BlockSpec((n_t, SB_DH * nh), lambda b, h: (b, h)),
        out_shape=jax.ShapeDtypeStruct((n_b * n_t, d), F32),
        scratch_shapes=[pltpu.VMEM((QB, SB_DH * nh), BF16), pltpu.VMEM((QB, SB_DH * nh), BF16)],
        compiler_params=_params(2),
        name="sb_prompt_attn",
    )(bias, qkv, qkv, qkv)


def _sbs_kernel(pt_ref, bias_ref, q_ref, kn_ref, vn_ref, *rest, n_heads, n_new, pp):
    kp_refs, vp_refs = rest[:pp], rest[pp:2 * pp]
    o_ref, cbc_scr, acc_scr, pad_scr = rest[2 * pp:]
    j = pl.program_id(1)
    w2 = _suffix_weights()
    hs = [slice(SB_DH * h, SB_DH * (h + 1)) for h in range(n_heads)]
    r = lax.broadcasted_iota(jnp.int32, (n_new * n_heads, PAGE), 0)
    c = lax.broadcasted_iota(jnp.int32, (n_new * n_heads, PAGE), 1)
    bias_col = jnp.concatenate([jnp.full((n_new, 1), bias_ref[h], F32) for h in range(n_heads)], axis=0)
    q = q_ref[0]
    qh = [q[n_new * h:n_new * (h + 1)].astype(BF16) for h in range(n_heads)]
    head_rows = lambda ref, h: ref[0, 0, pl.ds(h, PAGE, stride=n_heads), :].astype(BF16)

    def scores(k_at, mask):
        z = jnp.concatenate([lax.dot_general(qh[h], k_at(h), NT_DIMS, preferred_element_type=F32)
                             for h in range(n_heads)], axis=0) * (SB_DH ** -0.5) + bias_col
        ls, lsg = _sb_logits(z)
        if mask is not None:
            ls = jnp.where(mask, ls, 0.0)
        return lsg, _split_dot(ls, w2)

    def weigh(lsg, res, v_at, mask, cbc, acc):
        w = jnp.exp(lsg + res[:, :PAGE] + cbc)
        if mask is not None:
            w = jnp.where(mask, w, 0.0)
        pv = jnp.concatenate(
            [jnp.dot(w[n_new * h:n_new * (h + 1)].astype(BF16), v_at(h), preferred_element_type=F32)
             for h in range(n_heads)], axis=0)
        return cbc + res[:, PAGE:], acc + pv

    @pl.when(j == 0)
    def _():
        pad_scr[...] = jnp.zeros_like(pad_scr)
        pad_scr[0, 0:n_new, :] = kn_ref[0].astype(BF16)
        pad_scr[1, 0:n_new, :] = vn_ref[0].astype(BF16)
        mask = c < (r % n_new)
        zeros = jnp.zeros((n_new * n_heads, PAGE), F32)
        lsg, res = scores(lambda h: pad_scr[0, :, hs[h]], mask)
        cbc, acc = weigh(lsg, res, lambda h: pad_scr[1, :, hs[h]], mask, zeros, zeros)
        cbc_scr[...] = cbc
        acc_scr[...] = acc

    sc = [scores(functools.partial(head_rows, kp), None) for kp in kp_refs]
    cbc, acc = cbc_scr[...], acc_scr[...]
    for (lsg, res), vp in zip(sc, vp_refs):
        cbc, acc = weigh(lsg, res, functools.partial(head_rows, vp), None, cbc, acc)
    cbc_scr[...] = cbc
    acc_scr[...] = acc

    @pl.when(j == pl.num_programs(1) - 1)
    def _():
        o_ref[0] = acc


def sb_sample_attn(q_hq, k_new, v_new, cache_k, cache_v, layer, page_table, bias, pp):
    n_s, n_pages = page_table.shape
    n_new, d = k_new.shape[1:]
    n_heads = d // SB_DH
    assert n_pages % pp == 0
    page_specs = [pl.BlockSpec((1, 1, PAGE * n_heads, SB_DH),
                               lambda s, j, pt, b, i=i: (layer, pt[s, n_pages - 1 - (j * pp + i)], 0, 0))
                  for i in range(pp)]
    seq3 = lambda s, j, pt, b: (s, 0, 0)
    return pl.pallas_call(
        functools.partial(_sbs_kernel, n_heads=n_heads, n_new=n_new, pp=pp),
        grid_spec=pltpu.PrefetchScalarGridSpec(
            num_scalar_prefetch=2,
            grid=(n_s, n_pages // pp),
            in_specs=[pl.BlockSpec((1, n_new * n_heads, SB_DH), seq3),
                      pl.BlockSpec((1, n_new, d), seq3),
                      pl.BlockSpec((1, n_new, d), seq3)] + page_specs + page_specs,
            out_specs=pl.BlockSpec((1, n_new * n_heads, SB_DH), seq3),
            scratch_shapes=[pltpu.VMEM((n_new * n_heads, PAGE), F32), pltpu.VMEM((n_new * n_heads, SB_DH), F32),
                            pltpu.VMEM((2, PAGE, d), BF16)]),
        out_shape=jax.ShapeDtypeStruct((n_s, n_new * n_heads, SB_DH), F32),
        compiler_params=_params(2),
        name="sb_sample_attn",
    )(page_table, bias, q_hq, k_new, v_new, *([cache_k] * pp), *([cache_v] * pp))


def _rw_prep_kernel(k_ref, wl_ref, al_ref, w0_ref, a0_ref, kk_ref_, ka_ref, dec_o, kk_o, b_o, km_o):
    k = k_ref[...]
    w2 = _head_sum_weights(RW_DH)
    wpre = w0_ref[...] + wl_ref[...]
    sp = jnp.maximum(-wpre, 0.0) + jnp.log1p(jnp.exp(-jnp.abs(wpre)))
    dec_o[...] = jnp.exp(-jnp.exp(-sp - 0.5))
    a = jax.nn.sigmoid(a0_ref[...] + al_ref[...])
    kk = k * kk_ref_[...]
    kk = kk / jnp.maximum(jnp.sqrt(_head_sum(kk * kk, w2)), 1e-12)
    kk_o[...] = kk
    b_o[...] = kk * a
    km_o[...] = k * (1.0 + (a - 1.0) * ka_ref[...])


def rw_prep(k, wl, al, w0, a0, k_k, k_a):
    m, d = k.shape
    tm = min(256, m)
    tok = pl.BlockSpec((tm, d), lambda i: (i, 0))
    row = pl.BlockSpec((1, d), lambda i: (0, 0))
    out = jax.ShapeDtypeStruct((m, d), F32)
    return pl.pallas_call(
        _rw_prep_kernel,
        grid=(pl.cdiv(m, tm),),
        in_specs=[tok] * 3 + [row] * 4,
        out_specs=[tok] * 4,
        out_shape=[out] * 4,
        compiler_params=_params(1),
        name="rwkv_prep",
    )(k, wl, al, *(p.reshape(1, d) for p in (w0, a0, k_k, k_a)))


def _wkv_kernel(r_ref, d_ref, k_ref, v_ref, kk_ref, b_ref, s0_ref, y_ref, sT_ref, s_scr, *, n_pairs, tc):
    c = pl.program_id(1)

    @pl.when(c == 0)
    def _():
        for p in range(n_pairs):
            s_scr[RW_DH * p:RW_DH * (p + 1), :] = jnp.concatenate([s0_ref[0, 2 * p], s0_ref[0, 2 * p + 1]], axis=-1)

    row = lax.broadcasted_iota(jnp.int32, (RW_DH, 2 * RW_DH), 0)
    col = lax.broadcasted_iota(jnp.int32, (RW_DH, 2 * RW_DH), 1)
    sel = jnp.tile(row == (col % RW_DH), (n_pairs, 1))
    w2 = _head_sum_weights(RW_DH)

    def group(g, carry):
        t0 = pl.multiple_of(g * 8, 8)
        ld = lambda ref: ref[pl.ds(t0, 8), :]
        r8, d8, k8, v8, kk8, b8 = ld(r_ref), ld(d_ref), ld(k_ref), ld(v_ref), ld(kk_ref), ld(b_ref)
        s = s_scr[...]
        ys = []
        for j in range(8):
            def bc(x8):
                return jnp.concatenate(
                    [jnp.broadcast_to(x8[j:j + 1, 2 * RW_DH * p:2 * RW_DH * (p + 1)], (RW_DH, 2 * RW_DH))
                     for p in range(n_pairs)], axis=0)
            sa = _split_dot(s * bc(kk8), w2)
            vbc = _split_dot(jnp.where(sel, bc(v8), 0.0), w2)
            s = s * bc(d8) - sa * bc(b8) + vbc * bc(k8)
            ybc = _split_dot(s * bc(r8), w2)
            ys.append(jnp.sum(jnp.where(sel, ybc, 0.0).reshape(n_pairs, RW_DH, 2 * RW_DH), axis=1))
        s_scr[...] = s
        y_ref[pl.ds(t0, 8), :] = jnp.concatenate(
            [jnp.concatenate([ys[j][p:p + 1] for j in range(8)], axis=0) for p in range(n_pairs)], axis=1)
        return carry

    lax.fori_loop(0, tc // 8, group, 0)

    @pl.when(c == pl.num_programs(1) - 1)
    def _():
        for p in range(n_pairs):
            s = s_scr[RW_DH * p:RW_DH * (p + 1), :]
            sT_ref[0, 2 * p] = s[:, :RW_DH]
            sT_ref[0, 2 * p + 1] = s[:, RW_DH:]


def wkv_scan(r, dec, k, v, kk, b, s0, row0, n_t):
    d = r.shape[1]
    n_b, n_h = s0.shape[:2]
    tc = _divisor_tile(n_t, 48, 8)
    assert row0 % tc == 0
    nc = n_t // tc
    tok = pl.BlockSpec((tc, d), lambda i, c: (row0 // tc + i * nc + c, 0))
    st = pl.BlockSpec((1, n_h, RW_DH, RW_DH), lambda i, c: (i, 0, 0, 0))
    return pl.pallas_call(
        functools.partial(_wkv_kernel, n_pairs=n_h // 2, tc=tc),
        grid=(n_b, nc),
        in_specs=[tok] * 6 + [st],
        out_specs=[pl.BlockSpec((tc, d), lambda i, c: (i * nc + c, 0)), st],
        out_shape=[jax.ShapeDtypeStruct((n_b * n_t, d), F32), jax.ShapeDtypeStruct(s0.shape, F32)],
        scratch_shapes=[pltpu.VMEM((n_h // 2 * RW_DH, 2 * RW_DH), F32)],
        compiler_params=_params(2),
        name="wkv7_scan",
    )(r, dec, k, v, kk, b, s0)


def _rw_post_kernel(y_ref, r_ref, km_ref, v_ref, g_ref, rk_ref, lw_ref, lb_ref, o_ref):
    w2 = _head_sum_weights(RW_DH)
    y = y_ref[...]
    mean = _head_sum(y, w2) * (1.0 / RW_DH)
    yc = y - mean
    var = _head_sum(yc * yc, w2) * (1.0 / RW_DH)
    yn = yc * lax.rsqrt(var + GN_EPS) * lw_ref[...] + lb_ref[...]
    bonus = _head_sum(r_ref[...] * km_ref[...] * rk_ref[...], w2) * v_ref[...]
    o_ref[...] = (yn + bonus) * g_ref[...]


def rw_post(y, r, km, v, g, r_k, ln_w, ln_b):
    m, d = y.shape
    tm = min(256, m)
    tok = pl.BlockSpec((tm, d), lambda i: (i, 0))
    row = pl.BlockSpec((1, d), lambda i: (0, 0))
    return pl.pallas_call(
        _rw_post_kernel,
        grid=(pl.cdiv(m, tm),),
        in_specs=[tok] * 5 + [row] * 3,
        out_specs=tok,
        out_shape=jax.ShapeDtypeStruct((m, d), F32),
        compiler_params=_params(1),
        name="rwkv_post",
    )(y, r, km, v, g, *(p.reshape(1, d) for p in (r_k, ln_w, ln_b)))


def _moe_up_kernel(be_ref, nv_ref, x_ref, wg_ref, wu_ref, o_ref, wg_s, wu_s):
    i = pl.program_id(1)
    e = be_ref[i]
    prev = be_ref[jnp.maximum(i - 1, 0)]

    @pl.when((i == 0) | (e != prev))
    def _():
        wg_s[...] = wg_ref[0, 0].astype(BF16)
        wu_s[...] = wu_ref[0, 0].astype(BF16)

    @pl.when(i < nv_ref[0])
    def _():
        x = x_ref[...].astype(BF16)
        gate = jnp.dot(x, wg_s[...], preferred_element_type=F32)
        up = jnp.dot(x, wu_s[...], preferred_element_type=F32)
        o_ref[...] = gate * jax.nn.sigmoid(gate) * up

    @pl.when(i >= nv_ref[0])
    def _():
        o_ref[...] = jnp.zeros_like(o_ref)


def _moe_down_kernel(be_ref, nv_ref, h_ref, w_ref, o_ref, w_s):
    i = pl.program_id(1)
    e = be_ref[i]
    prev = be_ref[jnp.maximum(i - 1, 0)]

    @pl.when((i == 0) | (e != prev))
    def _():
        w_s[...] = w_ref[0, 0].astype(BF16)

    @pl.when(i < nv_ref[0])
    def _():
        o_ref[...] = jnp.dot(h_ref[...].astype(BF16), w_s[...], preferred_element_type=F32)

    @pl.when(i >= nv_ref[0])
    def _():
        o_ref[...] = jnp.zeros_like(o_ref)


def moe_experts(x_disp, block_e, n_valid, w_up, w_down, layer):
    n_rows, d = x_disp.shape
    n_blocks = n_rows // MOE_BLOCK
    f = w_down.shape[2]
    th = min(COL_TILE, f)
    nf = f // th
    hid = pl.pallas_call(
        _moe_up_kernel,
        grid_spec=pltpu.PrefetchScalarGridSpec(
            num_scalar_prefetch=2,
            grid=(nf, n_blocks),
            in_specs=[pl.BlockSpec((MOE_BLOCK, d), lambda j, i, be, nv: (i, 0)),
                      pl.BlockSpec((1, 1, d, th), lambda j, i, be, nv: (layer, be[i], 0, j)),
                      pl.BlockSpec((1, 1, d, th), lambda j, i, be, nv: (layer, be[i], 0, nf + j))],
            out_specs=pl.BlockSpec((MOE_BLOCK, th), lambda j, i, be, nv: (i, j)),
            scratch_shapes=[pltpu.VMEM((d, th), BF16), pltpu.VMEM((d, th), BF16)]),
        out_shape=jax.ShapeDtypeStruct((n_rows, f), F32),
        compiler_params=_params(2),
        name="moe_up_gate",
    )(block_e, n_valid, x_disp, w_up, w_up)
    tn = min(2 * COL_TILE, d)
    return pl.pallas_call(
        _moe_down_kernel,
        grid_spec=pltpu.PrefetchScalarGridSpec(
            num_scalar_prefetch=2,
            grid=(d // tn, n_blocks),
            in_specs=[pl.BlockSpec((MOE_BLOCK, f), lambda j, i, be, nv: (i, 0)),
                      pl.BlockSpec((1, 1, f, tn), lambda j, i, be, nv: (layer, be[i], 0, j))],
            out_specs=pl.BlockSpec((MOE_BLOCK, tn), lambda j, i, be, nv: (i, j)),
            scratch_shapes=[pltpu.VMEM((f, tn), BF16)]),
        out_shape=jax.ShapeDtypeStruct((n_rows, d), F32),
        compiler_params=_params(2),
        name="moe_down",
    )(block_e, n_valid, hid, w_down)


def hier_moe(h, tok, layer, w_group, b_group, w_expert, b_expert, w_up, w_down):
    n_tok, d = tok.shape
    n_groups = w_group.shape[-1]
    n_experts = w_expert.shape[-1]
    epg = n_experts // n_groups
    logits = matmul(tok, jnp.concatenate([w_group[layer], w_expert[layer]], axis=1))
    g_logit = logits[:, :n_groups] + b_group[layer]
    g_prob = jax.nn.softmax(g_logit, axis=-1)
    g_idx = jnp.argmax(g_logit, axis=-1)
    g_w = jnp.take_along_axis(g_prob, g_idx[:, None], axis=1)
    e_logit = (logits[:, n_groups:] + b_expert[layer]).reshape(n_tok, n_groups, epg)
    e_logit = jnp.take_along_axis(e_logit, g_idx[:, None, None], axis=1)[:, 0]
    top_p, top_i = lax.top_k(jax.nn.softmax(e_logit, axis=-1), TOP_K)
    gates = g_w * top_p / jnp.sum(top_p, axis=-1, keepdims=True)
    flat_e = (g_idx[:, None] * epg + top_i).reshape(-1).astype(jnp.int32)
    n_assign = n_tok * TOP_K
    order = jnp.argsort(flat_e)
    rank = jnp.argsort(order).astype(jnp.int32)
    counts = jnp.sum(flat_e[:, None] == jnp.arange(n_experts, dtype=jnp.int32)[None, :], axis=0, dtype=jnp.int32)
    starts = jnp.cumsum(counts) - counts
    padded = (counts + MOE_BLOCK - 1) // MOE_BLOCK * MOE_BLOCK
    pad_end = jnp.cumsum(padded)
    pad_start = pad_end - padded
    dest = (pad_start[flat_e] + rank - starts[flat_e]).astype(jnp.int32).reshape(n_tok, TOP_K)
    n_blocks = -(-n_assign // MOE_BLOCK) + n_experts
    block_start = jnp.arange(n_blocks, dtype=jnp.int32) * MOE_BLOCK
    block_e = jnp.minimum(jnp.sum(pad_end[None, :] <= block_start[:, None], axis=1), n_experts - 1).astype(jnp.int32)
    row = jnp.arange(n_blocks * MOE_BLOCK, dtype=jnp.int32)
    row_e = jnp.repeat(block_e, MOE_BLOCK)
    src = order[jnp.clip(starts[row_e] + row - pad_start[row_e], 0, n_assign - 1)]
    x_disp = tok[src // TOP_K]
    n_valid = (pad_end[-1:] // MOE_BLOCK).astype(jnp.int32)
    y_disp = moe_experts(x_disp, block_e, n_valid, w_up, w_down, layer)
    for kth in range(TOP_K):
        h = h + gates[:, kth:kth + 1] * y_disp[dest[:, kth]]
    return h


def kernel(x_prompt, x_sample, cache_k, cache_v, state_wkv, state_shift, page_table, meta_tokens, ln_mix, ln_ffn, sb_w_qkv, sb_q_norm, sb_k_norm, sb_logit_bias, sb_w_o, rw_mu, rw_w_r, rw_w_k, rw_w_v, rw_w_o, rw_w0, rw_w_la, rw_w_lb, rw_a0, rw_a_la, rw_a_lb, rw_g_la, rw_g_lb, rw_k_k, rw_k_a, rw_r_k, rw_ln_w, rw_ln_b, moe_w_group, moe_b_group, moe_w_expert, moe_b_expert, moe_w_up, moe_w_down):
    n_b, seq, d = x_prompt.shape
    n_s, n_new, _ = x_sample.shape
    n_t = N_META + seq
    n_p = n_b * n_t
    sb_heads = d // SB_DH
    rw_heads = d // RW_DH
    depth = ln_mix.shape[0]
    moe = (moe_w_group, moe_b_group, moe_w_expert, moe_b_expert, moe_w_up, moe_w_down)

    meta = jnp.broadcast_to(meta_tokens[None], (n_b, N_META, d))
    h = jnp.concatenate([jnp.concatenate([meta, x_prompt], axis=1).reshape(n_p, d), x_sample.reshape(n_s * n_new, d)],
                        axis=0)
    pool = cache_k.shape[1]
    cache_k2 = cache_k.reshape(cache_k.shape[0], pool, PAGE * sb_heads, SB_DH)
    cache_v2 = cache_v.reshape(cache_v.shape[0], pool, PAGE * sb_heads, SB_DH)

    k_p, v_p, k_s, v_s, wkv_p, wkv_s, sh_p, sh_s = [], [], [], [], [], [], [], []
    for i in range(depth):
        li = i // 2
        xn = rmsnorm(h, ln_mix[i])
        if i % 2 == 0:
            qkv = matmul_qkv(xn, sb_w_qkv[li], sb_q_norm[li], sb_k_norm[li])
            kx, vx = qkv[n_p:, d:2 * d], qkv[n_p:, 2 * d:]
            k3, v3 = kv_head_rows(qkv, 0, n_p)
            k_p.append(k3.reshape(n_b, n_t, sb_heads, SB_DH))
            v_p.append(v3.reshape(n_b, n_t, sb_heads, SB_DH))
            k3, v3 = kv_head_rows(qkv, n_p, n_s * n_new)
            k_s.append(k3.reshape(n_s, n_new, sb_heads, SB_DH))
            v_s.append(v3.reshape(n_s, n_new, sb_heads, SB_DH))
            o_p = sb_prompt_attn(qkv, sb_logit_bias[li], n_b, n_t, nh=min(4, sb_heads))
            q_hq = qkv[n_p:, :d].reshape(n_s, n_new, sb_heads, SB_DH).transpose(0, 2, 1, 3)
            o_s = sb_sample_attn(q_hq.reshape(n_s, sb_heads * n_new, SB_DH), kx.reshape(n_s, n_new, d),
                                 vx.reshape(n_s, n_new, d), cache_k2, cache_v2, li, page_table,
                                 sb_logit_bias[li], pp=math.gcd(page_table.shape[1], 4))
            o_s = o_s.reshape(n_s, sb_heads, n_new, SB_DH).transpose(0, 2, 1, 3).reshape(n_s * n_new, d)
            h = matmul_residual(jnp.concatenate([o_p, o_s], axis=0), sb_w_o[li], h)
        else:
            xp3 = xn[:n_p].reshape(n_b, n_t, d)
            xs3 = xn[n_p:].reshape(n_s, n_new, d)
            sh_p.append(xp3[:, -1])
            sh_s.append(xs3[:, -1])
            x_prev = jnp.concatenate(
                [jnp.concatenate([jnp.zeros((n_b, 1, d), F32), xp3[:, :-1]], axis=1).reshape(n_p, d),
                 jnp.concatenate([state_shift[li][:, None], xs3[:, :-1]], axis=1).reshape(n_s * n_new, d)], axis=0)
            mu = rw_mu[li]
            r = matmul_mix(xn, x_prev, mu[0], rw_w_r[li])
            k = matmul_mix(xn, x_prev, mu[2], rw_w_k[li])
            v = matmul_mix(xn, x_prev, mu[3], rw_w_v[li])
            wl = matmul(matmul_mix(xn, x_prev, mu[1], rw_w_la[li], act="tanh"), rw_w_lb[li])
            al = matmul(matmul_mix(xn, x_prev, mu[4], rw_a_la[li]), rw_a_lb[li])
            g = matmul(matmul_mix(xn, x_prev, mu[5], rw_g_la[li], act="sigmoid"), rw_g_lb[li])
            dec, kk, b, km = rw_prep(k, wl, al, rw_w0[li], rw_a0[li], rw_k_k[li], rw_k_a[li])
            y_p, s_p = wkv_scan(r, dec, km, v, kk, b, jnp.zeros((n_b, rw_heads, RW_DH, RW_DH), F32), 0, n_t)
            y_s, s_s = wkv_scan(r, dec, km, v, kk, b, state_wkv[li], n_p, n_new)
            wkv_p.append(s_p)
            wkv_s.append(s_s)
            z = rw_post(jnp.concatenate([y_p, y_s], axis=0), r, km, v, g, rw_r_k[li].reshape(-1), rw_ln_w[li],
                        rw_ln_b[li])
            h = matmul_residual(z, rw_w_o[li], h)
        h = hier_moe(h, rmsnorm(h, ln_ffn[i]), i, *moe)
    return (h[:n_p].reshape(n_b, n_t, d)[:, N_META:], h[n_p:].reshape(n_s, n_new, d),
            jnp.stack(k_p), jnp.stack(v_p), jnp.stack(k_s), jnp.stack(v_s),
            jnp.stack(wkv_p), jnp.stack(wkv_s), jnp.stack(sh_p), jnp.stack(sh_s))
```

```python
import functools
import math

import jax
import jax.numpy as jnp
from jax import lax
from jax.experimental import pallas as pl
from jax.experimental.pallas import tpu as pltpu

N_META = 16
RMS_EPS = 1e-6
GN_EPS = 64e-5
SB_DH = 128
QB = 128
PAGE = 128
RW_DH = 64
TOP_K = 2
MOE_BLOCK = 128

F32 = jnp.float32
BF16 = jnp.bfloat16
NT_DIMS = (((1,), (1,)), ((), ()))

VMEM_LIMIT = 56 * 1024 * 1024
ROW_TILE = 512
COL_TILE = 1024


def _params(n_axes):
    return pltpu.CompilerParams(dimension_semantics=("arbitrary",) * n_axes, vmem_limit_bytes=VMEM_LIMIT)


def _divisor_tile(n, cap, mult):
    best = None
    for t in range(mult, min(n, cap) + 1, mult):
        if n % t == 0:
            best = t
    assert best is not None, (n, cap, mult)
    return best


def _split_dot(x, w2):
    hi = x.astype(BF16)
    lo = (x - hi.astype(F32)).astype(BF16)
    return jnp.dot(jnp.concatenate([hi, lo], axis=1), w2, preferred_element_type=F32)


def _head_sum_weights(dh):
    r = lax.broadcasted_iota(jnp.int32, (256, 128), 0)
    c = lax.broadcasted_iota(jnp.int32, (256, 128), 1)
    return (((r % 128) // dh) == (c // dh)).astype(BF16)


def _head_sum(x, w2):
    return jnp.concatenate([_split_dot(x[:, c:c + 128], w2) for c in range(0, x.shape[1], 128)], axis=1)


def _rmsnorm_kernel(x_ref, g_ref, o_ref):
    x = x_ref[...]
    o_ref[...] = x * lax.rsqrt(jnp.mean(x * x, axis=-1, keepdims=True) + RMS_EPS) * g_ref[...]


def rmsnorm(x, gain):
    m, d = x.shape
    tm = min(ROW_TILE, m)
    return pl.pallas_call(
        _rmsnorm_kernel,
        grid=(pl.cdiv(m, tm),),
        in_specs=[pl.BlockSpec((tm, d), lambda i: (i, 0)), pl.BlockSpec((1, d), lambda i: (0, 0))],
        out_specs=pl.BlockSpec((tm, d), lambda i: (i, 0)),
        out_shape=jax.ShapeDtypeStruct((m, d), F32),
        compiler_params=_params(1),
        name="rmsnorm",
    )(x, gain.reshape(1, d))


def _bdot(x, w_ref, w_s):
    @pl.when(pl.program_id(1) == 0)
    def _():
        w_s[...] = w_ref[...].astype(BF16)
    return jnp.dot(x.astype(BF16), w_s[...], preferred_element_type=F32)


def _mm_kernel(x_ref, w_ref, o_ref, w_s):
    o_ref[...] = _bdot(x_ref[...], w_ref, w_s)


def _mm_res_kernel(x_ref, w_ref, r_ref, o_ref, w_s):
    o_ref[...] = r_ref[...] + _bdot(x_ref[...], w_ref, w_s)


def _mm_mix_kernel(x_ref, xp_ref, mu_ref, w_ref, o_ref, w_s, *, act):
    x = x_ref[...]
    y = _bdot(x + (xp_ref[...] - x) * mu_ref[...], w_ref, w_s)
    if act == "tanh":
        y = jnp.tanh(y)
    elif act == "sigmoid":
        y = jax.nn.sigmoid(y)
    o_ref[...] = y


def _mm_qkv_kernel(x_ref, w_ref, gq_ref, gk_ref, o_ref, w_s, *, n_q_tiles):
    j = pl.program_id(0)
    y = _bdot(x_ref[...], w_ref, w_s)
    gain = jnp.where(j < n_q_tiles, gq_ref[...], gk_ref[...])
    cols = []
    for c in range(0, y.shape[1], SB_DH):
        yh = y[:, c:c + SB_DH]
        cols.append(yh * lax.rsqrt(jnp.mean(yh * yh, axis=-1, keepdims=True) + RMS_EPS) * gain)
    o_ref[...] = jnp.where(j < 2 * n_q_tiles, jnp.concatenate(cols, axis=1), y)


def _matmul_call(kern, x_like, w, extra_in, extra_specs, name, tn=None):
    m, k = x_like[0].shape
    n = w.shape[1]
    tm, tn = min(ROW_TILE, m), tn or min(COL_TILE, n)
    xs = pl.BlockSpec((tm, k), lambda j, i: (i, 0))
    return pl.pallas_call(
        kern,
        grid=(pl.cdiv(n, tn), pl.cdiv(m, tm)),
        in_specs=[xs] * len(x_like) + extra_specs(tm, tn, k),
        out_specs=pl.BlockSpec((tm, tn), lambda j, i: (i, j)),
        out_shape=jax.ShapeDtypeStruct((m, n), F32),
        scratch_shapes=[pltpu.VMEM((k, tn), BF16)],
        compiler_params=_params(2),
        name=name,
    )(*x_like, *extra_in(w))


def _w_spec(tn, k):
    return pl.BlockSpec((k, tn), lambda j, i: (0, j))


def matmul(x, w):
    return _matmul_call(_mm_kernel, [x], w, lambda w: [w], lambda tm, tn, k: [_w_spec(tn, k)], "dense_matmul")


def matmul_residual(x, w, res):
    return _matmul_call(_mm_res_kernel, [x], w, lambda w: [w, res],
                        lambda tm, tn, k: [_w_spec(tn, k), pl.BlockSpec((tm, tn), lambda j, i: (i, j))],
                        "dense_matmul_residual")


def matmul_mix(x, x_prev, mu, w, act=None):
    return _matmul_call(functools.partial(_mm_mix_kernel, act=act), [x, x_prev], w,
                        lambda w: [mu.reshape(1, -1), w],
                        lambda tm, tn, k: [pl.BlockSpec((1, k), lambda j, i: (0, 0)), _w_spec(tn, k)],
                        "token_shift_matmul")


def matmul_qkv(x, w, g_q, g_k):
    d = x.shape[1]
    tn = min(COL_TILE, d)
    assert d % tn == 0 and tn % SB_DH == 0
    gspec = lambda tm, tn_, k: [_w_spec(tn_, k), pl.BlockSpec((1, SB_DH), lambda j, i: (0, 0)),
                                pl.BlockSpec((1, SB_DH), lambda j, i: (0, 0))]
    return _matmul_call(functools.partial(_mm_qkv_kernel, n_q_tiles=d // tn), [x], w,
                        lambda w: [w, g_q.reshape(1, SB_DH), g_k.reshape(1, SB_DH)], gspec, "qkv_headnorm_matmul",
                        tn=tn)


def _head_rows_kernel(k_ref, v_ref, ko_ref, vo_ref, *, n_heads):
    tm = k_ref.shape[0]
    for h in range(n_heads):
        cols = slice(SB_DH * h, SB_DH * (h + 1))
        ko_ref[pl.ds(h, tm, stride=n_heads), :] = k_ref[:, cols]
        vo_ref[pl.ds(h, tm, stride=n_heads), :] = v_ref[:, cols]


def kv_head_rows(qkv, row0, n_rows):
    d = qkv.shape[1] // 3
    n_heads = d // SB_DH
    tm = _divisor_tile(math.gcd(row0, n_rows), 256, 8)
    out = jax.ShapeDtypeStruct((n_rows * n_heads, SB_DH), F32)
    ospec = pl.BlockSpec((tm * n_heads, SB_DH), lambda i: (i, 0))
    return pl.pallas_call(
        functools.partial(_head_rows_kernel, n_heads=n_heads),
        grid=(n_rows // tm,),
        in_specs=[pl.BlockSpec((tm, d), lambda i: (row0 // tm + i, 1)), pl.BlockSpec((tm, d), lambda i: (row0 // tm + i, 2))],
        out_specs=[ospec, ospec],
        out_shape=[out, out],
        compiler_params=_params(1),
        name="kv_head_rows",
    )(qkv, qkv)


def _suffix_weights():
    r = lax.broadcasted_iota(jnp.int32, (QB, QB), 0)
    c = lax.broadcasted_iota(jnp.int32, (QB, QB), 1)
    w = jnp.concatenate([(r > c).astype(BF16), jnp.ones((QB, QB), BF16)], axis=1)
    return jnp.concatenate([w, w], axis=0)


def _sb_logits(z):
    ls = -(jnp.maximum(z, 0.0) + jnp.log1p(jnp.exp(-jnp.abs(z))))
    return ls, ls + z


def _sb_scores(qbs, kbs, mask, biases, w2):
    lss, lsgs = [], []
    for qb, kb, bias in zip(qbs, kbs, biases):
        z = lax.dot_general(qb, kb, NT_DIMS, preferred_element_type=F32) * (SB_DH ** -0.5) + bias
        ls, lsg = _sb_logits(z)
        lsgs.append(lsg)
        lss.append(ls if mask is None else jnp.where(mask, ls, 0.0))
    return lsgs, _split_dot(jnp.concatenate(lss, axis=0), w2)


def _sb_weigh(lsgs, res, vbs, mask, cbcs, accs):
    new_cbcs, new_accs = [], []
    for i in range(len(vbs)):
        ri = res[QB * i:QB * (i + 1)]
        w = jnp.exp(lsgs[i] + ri[:, :QB] + cbcs[i])
        if mask is not None:
            w = jnp.where(mask, w, 0.0)
        new_accs.append(accs[i] + jnp.dot(w.astype(BF16), vbs[i], preferred_element_type=F32))
        new_cbcs.append(cbcs[i] + ri[:, QB:])
    return tuple(new_cbcs), tuple(new_accs)


def _sb_tile(qbs, kbs, vbs, mask, biases, w2, cbcs, accs):
    lsgs, res = _sb_scores(qbs, kbs, mask, biases, w2)
    return _sb_weigh(lsgs, res, vbs, mask, cbcs, accs)


def _sbp_kernel(bias_ref, q_ref, k_ref, v_ref, o_ref, kpad, vpad, *, n_blk, nh):
    h0 = pl.program_id(1) * nh
    biases = [bias_ref[h0 + i] for i in range(nh)]
    hs = [slice(SB_DH * i, SB_DH * (i + 1)) for i in range(nh)]
    w2 = _suffix_weights()
    r = lax.broadcasted_iota(jnp.int32, (QB, QB), 0)
    c = lax.broadcasted_iota(jnp.int32, (QB, QB), 1)
    diag_mask = c < r
    meta_mask = c < N_META
    zeros = tuple(jnp.zeros((QB, QB), F32) for _ in range(nh))

    kpad[...] = jnp.zeros_like(kpad)
    vpad[...] = jnp.zeros_like(vpad)
    kpad[0:N_META, :] = k_ref[0:N_META, :].astype(BF16)
    vpad[0:N_META, :] = v_ref[0:N_META, :].astype(BF16)
    rows = lambda ref, r0: [ref[pl.ds(r0, QB), s].astype(BF16) for s in hs]
    kms = [kpad[:, s] for s in hs]
    vms = [vpad[:, s] for s in hs]

    _, accs = _sb_tile(rows(q_ref, 0), kms, vms, diag_mask & meta_mask, biases, w2, zeros, zeros)
    for i in range(nh):
        o_ref[0:N_META, hs[i]] = accs[i][0:N_META]

    def q_block(i, carry):
        q0 = pl.multiple_of(N_META + QB * i, 8)
        qbs = rows(q_ref, q0)
        ca = _sb_tile(qbs, rows(k_ref, q0), rows(v_ref, q0), diag_mask, biases, w2, zeros, zeros)

        def k_pair(j, ca):
            m0 = i - 1 - 2 * j
            m1 = m0 - 1
            k0 = pl.multiple_of(N_META + QB * m0, 8)
            k1 = pl.multiple_of(N_META + QB * jnp.maximum(m1, 0), 8)
            real = jnp.full((QB, QB), m1, jnp.int32) >= 0
            s0 = _sb_scores(qbs, rows(k_ref, k0), None, biases, w2)
            s1 = _sb_scores(qbs, rows(k_ref, k1), real, biases, w2)
            ca = _sb_weigh(*s0, rows(v_ref, k0), None, *ca)
            return _sb_weigh(*s1, rows(v_ref, k1), real, *ca)

        ca = lax.fori_loop(0, (i + 1) // 2, k_pair, ca)
        _, accs = _sb_tile(qbs, kms, vms, meta_mask, biases, w2, *ca)
        for hh in range(nh):
            o_ref[pl.ds(q0, QB), hs[hh]] = accs[hh]
        return carry

    lax.fori_loop(0, n_blk, q_block, 0)


def sb_prompt_attn(qkv, bias, n_b, n_t, nh):
    d = qkv.shape[1] // 3
    hb = d // (SB_DH * nh)
    blk = lambda off: pl.BlockSpec((n_t, SB_DH * nh), lambda b, h, off=off: (b, off + h))
    return pl.pallas_call(
        functools.partial(_sbp_kernel, n_blk=(n_t - N_META) // QB, nh=nh),
        grid=(n_b, hb),
        in_specs=[pl.BlockSpec(memory_space=pltpu.SMEM), blk(0), blk(hb), blk(2 * hb)],
        out_specs=pl.BlockSpec((n_t, SB_DH * nh), lambda b, h: (b, h)),
        out_shape=jax.ShapeDtypeStruct((n_b * n_t, d), F32),
        scratch_shapes=[pltpu.VMEM((QB, SB_DH * nh), BF16), pltpu.VMEM((QB, SB_DH * nh), BF16)],
        compiler_params=_params(2),
        name="sb_prompt_attn",
    )(bias, qkv, qkv, qkv)


def _sbs_kernel(pt_ref, bias_ref, q_ref, kn_ref, vn_ref, *rest, n_heads, n_new, pp):
    kp_refs, vp_refs = rest[:pp], rest[pp:2 * pp]
    o_ref, cbc_scr, acc_scr, pad_scr = rest[2 * pp:]
    j = pl.program_id(1)
    w2 = _suffix_weights()
    hs = [slice(SB_DH * h, SB_DH * (h + 1)) for h in range(n_heads)]
    r = lax.broadcasted_iota(jnp.int32, (n_new * n_heads, PAGE), 0)
    c = lax.broadcasted_iota(jnp.int32, (n_new * n_heads, PAGE), 1)
    bias_col = jnp.concatenate([jnp.full((n_new, 1), bias_ref[h], F32) for h in range(n_heads)], axis=0)
    q = q_ref[0]
    qh = [q[n_new * h:n_new * (h + 1)].astype(BF16) for h in range(n_heads)]
    head_rows = lambda ref, h: ref[0, 0, pl.ds(h, PAGE, stride=n_heads), :].astype(BF16)

    def scores(k_at, mask):
        z = jnp.concatenate([lax.dot_general(qh[h], k_at(h), NT_DIMS, preferred_element_type=F32)
                             for h in range(n_heads)], axis=0) * (SB_DH ** -0.5) + bias_col
        ls, lsg = _sb_logits(z)
        if mask is not None:
            ls = jnp.where(mask, ls, 0.0)
        return lsg, _split_dot(ls, w2)

    def weigh(lsg, res, v_at, mask, cbc, acc):
        w = jnp.exp(lsg + res[:, :PAGE] + cbc)
        if mask is not None:
            w = jnp.where(mask, w, 0.0)
        pv = jnp.concatenate(
            [jnp.dot(w[n_new * h:n_new * (h + 1)].astype(BF16), v_at(h), preferred_element_type=F32)
             for h in range(n_heads)], axis=0)
        return cbc + res[:, PAGE:], acc + pv

    @pl.when(j == 0)
    def _():
        pad_scr[...] = jnp.zeros_like(pad_scr)
        pad_scr[0, 0:n_new, :] = kn_ref[0].astype(BF16)
        pad_scr[1, 0:n_new, :] = vn_ref[0].astype(BF16)
        mask = c < (r % n_new)
        zeros = jnp.zeros((n_new * n_heads, PAGE), F32)
        lsg, res = scores(lambda h: pad_scr[0, :, hs[h]], mask)
        cbc, acc = weigh(lsg, res, lambda h: pad_scr[1, :, hs[h]], mask, zeros, zeros)
        cbc_scr[...] = cbc
        acc_scr[...] = acc

    sc = [scores(functools.partial(head_rows, kp), None) for kp in kp_refs]
    cbc, acc = cbc_scr[...], acc_scr[...]
    for (lsg, res), vp in zip(sc, vp_refs):
        cbc, acc = weigh(lsg, res, functools.partial(head_rows, vp), None, cbc, acc)
    cbc_scr[...] = cbc
    acc_scr[...] = acc

    @pl.when(j == pl.num_programs(1) - 1)
    def _():
        o_ref[0] = acc


def sb_sample_attn(q_hq, k_new, v_new, cache_k, cache_v, layer, page_table, bias, pp):
    n_s, n_pages = page_table.shape
    n_new, d = k_new.shape[1:]
    n_heads = d // SB_DH
    assert n_pages % pp == 0
    page_specs = [pl.BlockSpec((1, 1, PAGE * n_heads, SB_DH),
                               lambda s, j, pt, b, i=i: (layer, pt[s, n_pages - 1 - (j * pp + i)], 0, 0))
                  for i in range(pp)]
    seq3 = lambda s, j, pt, b: (s, 0, 0)
    return pl.pallas_call(
        functools.partial(_sbs_kernel, n_heads=n_heads, n_new=n_new, pp=pp),
        grid_spec=pltpu.PrefetchScalarGridSpec(
            num_scalar_prefetch=2,
            grid=(n_s, n_pages // pp),
            in_specs=[pl.BlockSpec((1, n_new * n_heads, SB_DH), seq3),
                      pl.BlockSpec((1, n_new, d), seq3),
                      pl.BlockSpec((1, n_new, d), seq3)] + page_specs + page_specs,
            out_specs=pl.BlockSpec((1, n_new * n_heads, SB_DH), seq3),
            scratch_shapes=[pltpu.VMEM((n_new * n_heads, PAGE), F32), pltpu.VMEM((n_new * n_heads, SB_DH), F32),
                            pltpu.VMEM((2, PAGE, d), BF16)]),
        out_shape=jax.ShapeDtypeStruct((n_s, n_new * n_heads, SB_DH), F32),
        compiler_params=_params(2),
        name="sb_sample_attn",
    )(page_table, bias, q_hq, k_new, v_new, *([cache_k] * pp), *([cache_v] * pp))


def _rw_prep_kernel(k_ref, wl_ref, al_ref, w0_ref, a0_ref, kk_ref_, ka_ref, dec_o, kk_o, b_o, km_o):
    k = k_ref[...]
    w2 = _head_sum_weights(RW_DH)
    wpre = w0_ref[...] + wl_ref[...]
    sp = jnp.maximum(-wpre, 0.0) + jnp.log1p(jnp.exp(-jnp.abs(wpre)))
    dec_o[...] = jnp.exp(-jnp.exp(-sp - 0.5))
    a = jax.nn.sigmoid(a0_ref[...] + al_ref[...])
    kk = k * kk_ref_[...]
    kk = kk / jnp.maximum(jnp.sqrt(_head_sum(kk * kk, w2)), 1e-12)
    kk_o[...] = kk
    b_o[...] = kk * a
    km_o[...] = k * (1.0 + (a - 1.0) * ka_ref[...])


def rw_prep(k, wl, al, w0, a0, k_k, k_a):
    m, d = k.shape
    tm = min(256, m)
    tok = pl.BlockSpec((tm, d), lambda i: (i, 0))
    row = pl.BlockSpec((1, d), lambda i: (0, 0))
    out = jax.ShapeDtypeStruct((m, d), F32)
    return pl.pallas_call(
        _rw_prep_kernel,
        grid=(pl.cdiv(m, tm),),
        in_specs=[tok] * 3 + [row] * 4,
        out_specs=[tok] * 4,
        out_shape=[out] * 4,
        compiler_params=_params(1),
        name="rwkv_prep",
    )(k, wl, al, *(p.reshape(1, d) for p in (w0, a0, k_k, k_a)))


def _wkv_kernel(r_ref, d_ref, k_ref, v_ref, kk_ref, b_ref, s0_ref, y_ref, sT_ref, s_scr, *, n_pairs, tc):
    c = pl.program_id(1)

    @pl.when(c == 0)
    def _():
        for p in range(n_pairs):
            s_scr[RW_DH * p:RW_DH * (p + 1), :] = jnp.concatenate([s0_ref[0, 2 * p], s0_ref[0, 2 * p + 1]], axis=-1)

    row = lax.broadcasted_iota(jnp.int32, (RW_DH, 2 * RW_DH), 0)
    col = lax.broadcasted_iota(jnp.int32, (RW_DH, 2 * RW_DH), 1)
    sel = jnp.tile(row == (col % RW_DH), (n_pairs, 1))
    w2 = _head_sum_weights(RW_DH)

    def group(g, carry):
        t0 = pl.multiple_of(g * 8, 8)
        ld = lambda ref: ref[pl.ds(t0, 8), :]
        r8, d8, k8, v8, kk8, b8 = ld(r_ref), ld(d_ref), ld(k_ref), ld(v_ref), ld(kk_ref), ld(b_ref)
        s = s_scr[...]
        vT = [v8[:, 2 * RW_DH * p:2 * RW_DH * (p + 1)].T for p in range(n_pairs)]
        ys = []
        for j in range(8):
            def bc(x8):
                return jnp.concatenate(
                    [jnp.broadcast_to(x8[j:j + 1, 2 * RW_DH * p:2 * RW_DH * (p + 1)], (RW_DH, 2 * RW_DH))
                     for p in range(n_pairs)], axis=0)
            sa = _split_dot(s * bc(kk8), w2)
            vbc = jnp.concatenate(
                [jnp.concatenate([jnp.broadcast_to(vT[p][RW_DH * hh:RW_DH * (hh + 1), j:j + 1], (RW_DH, RW_DH))
                                  for hh in range(2)], axis=1) for p in range(n_pairs)], axis=0)
            s = s * bc(d8) - sa * bc(b8) + vbc * bc(k8)
            ybc = _split_dot(s * bc(r8), w2)
            ys.append(jnp.sum(jnp.where(sel, ybc, 0.0).reshape(n_pairs, RW_DH, 2 * RW_DH), axis=1))
        s_scr[...] = s
        y_ref[pl.ds(t0, 8), :] = jnp.concatenate(
            [jnp.concatenate([ys[j][p:p + 1] for j in range(8)], axis=0) for p in range(n_pairs)], axis=1)
        return carry

    lax.fori_loop(0, tc // 8, group, 0)

    @pl.when(c == pl.num_programs(1) - 1)
    def _():
        for p in range(n_pairs):
            s = s_scr[RW_DH * p:RW_DH * (p + 1), :]
            sT_ref[0, 2 * p] = s[:, :RW_DH]
            sT_ref[0, 2 * p + 1] = s[:, RW_DH:]


def wkv_scan(r, dec, k, v, kk, b, s0, row0, n_t):
    d = r.shape[1]
    n_b, n_h = s0.shape[:2]
    tc = _divisor_tile(n_t, 48, 8)
    assert row0 % tc == 0
    nc = n_t // tc
    tok = pl.BlockSpec((tc, d), lambda i, c: (row0 // tc + i * nc + c, 0))
    st = pl.BlockSpec((1, n_h, RW_DH, RW_DH), lambda i, c: (i, 0, 0, 0))
    return pl.pallas_call(
        functools.partial(_wkv_kernel, n_pairs=n_h // 2, tc=tc),
        grid=(n_b, nc),
        in_specs=[tok] * 6 + [st],
        out_specs=[pl.BlockSpec((tc, d), lambda i, c: (i * nc + c, 0)), st],
        out_shape=[jax.ShapeDtypeStruct((n_b * n_t, d), F32), jax.ShapeDtypeStruct(s0.shape, F32)],
        scratch_shapes=[pltpu.VMEM((n_h // 2 * RW_DH, 2 * RW_DH), F32)],
        compiler_params=_params(2),
        name="wkv7_scan",
    )(r, dec, k, v, kk, b, s0)


def _rw_post_kernel(y_ref, r_ref, km_ref, v_ref, g_ref, rk_ref, lw_ref, lb_ref, o_ref):
    w2 = _head_sum_weights(RW_DH)
    y = y_ref[...]
    mean = _head_sum(y, w2) * (1.0 / RW_DH)
    yc = y - mean
    var = _head_sum(yc * yc, w2) * (1.0 / RW_DH)
    yn = yc * lax.rsqrt(var + GN_EPS) * lw_ref[...] + lb_ref[...]
    bonus = _head_sum(r_ref[...] * km_ref[...] * rk_ref[...], w2) * v_ref[...]
    o_ref[...] = (yn + bonus) * g_ref[...]


def rw_post(y, r, km, v, g, r_k, ln_w, ln_b):
    m, d = y.shape
    tm = min(256, m)
    tok = pl.BlockSpec((tm, d), lambda i: (i, 0))
    row = pl.BlockSpec((1, d), lambda i: (0, 0))
    return pl.pallas_call(
        _rw_post_kernel,
        grid=(pl.cdiv(m, tm),),
        in_specs=[tok] * 5 + [row] * 3,
        out_specs=tok,
        out_shape=jax.ShapeDtypeStruct((m, d), F32),
        compiler_params=_params(1),
        name="rwkv_post",
    )(y, r, km, v, g, *(p.reshape(1, d) for p in (r_k, ln_w, ln_b)))


def _moe_up_kernel(be_ref, nv_ref, x_ref, wg_ref, wu_ref, o_ref, wg_s, wu_s):
    i = pl.program_id(1)
    e = be_ref[i]
    prev = be_ref[jnp.maximum(i - 1, 0)]

    @pl.when((i == 0) | (e != prev))
    def _():
        wg_s[...] = wg_ref[0, 0].astype(BF16)
        wu_s[...] = wu_ref[0, 0].astype(BF16)

    @pl.when(i < nv_ref[0])
    def _():
        x = x_ref[...].astype(BF16)
        gate = jnp.dot(x, wg_s[...], preferred_element_type=F32)
        up = jnp.dot(x, wu_s[...], preferred_element_type=F32)
        o_ref[...] = gate * jax.nn.sigmoid(gate) * up

    @pl.when(i >= nv_ref[0])
    def _():
        o_ref[...] = jnp.zeros_like(o_ref)


def _moe_down_kernel(be_ref, nv_ref, h_ref, w_ref, o_ref, w_s):
    i = pl.program_id(1)
    e = be_ref[i]
    prev = be_ref[jnp.maximum(i - 1, 0)]

    @pl.when((i == 0) | (e != prev))
    def _():
        w_s[...] = w_ref[0, 0].astype(BF16)

    @pl.when(i < nv_ref[0])
    def _():
        o_ref[...] = jnp.dot(h_ref[...].astype(BF16), w_s[...], preferred_element_type=F32)

    @pl.when(i >= nv_ref[0])
    def _():
        o_ref[...] = jnp.zeros_like(o_ref)


def moe_experts(x_disp, block_e, n_valid, w_up, w_down, layer):
    n_rows, d = x_disp.shape
    n_blocks = n_rows // MOE_BLOCK
    f = w_down.shape[2]
    th = min(COL_TILE, f)
    nf = f // th
    hid = pl.pallas_call(
        _moe_up_kernel,
        grid_spec=pltpu.PrefetchScalarGridSpec(
            num_scalar_prefetch=2,
            grid=(nf, n_blocks),
            in_specs=[pl.BlockSpec((MOE_BLOCK, d), lambda j, i, be, nv: (i, 0)),
                      pl.BlockSpec((1, 1, d, th), lambda j, i, be, nv: (layer, be[i], 0, j)),
                      pl.BlockSpec((1, 1, d, th), lambda j, i, be, nv: (layer, be[i], 0, nf + j))],
            out_specs=pl.BlockSpec((MOE_BLOCK, th), lambda j, i, be, nv: (i, j)),
            scratch_shapes=[pltpu.VMEM((d, th), BF16), pltpu.VMEM((d, th), BF16)]),
        out_shape=jax.ShapeDtypeStruct((n_rows, f), F32),
        compiler_params=_params(2),
        name="moe_up_gate",
    )(block_e, n_valid, x_disp, w_up, w_up)
    tn = min(2 * COL_TILE, d)
    return pl.pallas_call(
        _moe_down_kernel,
        grid_spec=pltpu.PrefetchScalarGridSpec(
            num_scalar_prefetch=2,
            grid=(d // tn, n_blocks),
            in_specs=[pl.BlockSpec((MOE_BLOCK, f), lambda j, i, be, nv: (i, 0)),
                      pl.BlockSpec((1, 1, f, tn), lambda j, i, be, nv: (layer, be[i], 0, j))],
            out_specs=pl.BlockSpec((MOE_BLOCK, tn), lambda j, i, be, nv: (i, j)),
            scratch_shapes=[pltpu.VMEM((f, tn), BF16)]),
        out_shape=jax.ShapeDtypeStruct((n_rows, d), F32),
        compiler_params=_params(2),
        name="moe_down",
    )(block_e, n_valid, hid, w_down)


def hier_moe(h, tok, layer, w_group, b_group, w_expert, b_expert, w_up, w_down):
    n_tok, d = tok.shape
    n_groups = w_group.shape[-1]
    n_experts = w_expert.shape[-1]
    epg = n_experts // n_groups
    logits = matmul(tok, jnp.concatenate([w_group[layer], w_expert[layer]], axis=1))
    g_logit = logits[:, :n_groups] + b_group[layer]
    g_prob = jax.nn.softmax(g_logit, axis=-1)
    g_idx = jnp.argmax(g_logit, axis=-1)
    g_w = jnp.take_along_axis(g_prob, g_idx[:, None], axis=1)
    e_logit = (logits[:, n_groups:] + b_expert[layer]).reshape(n_tok, n_groups, epg)
    e_logit = jnp.take_along_axis(e_logit, g_idx[:, None, None], axis=1)[:, 0]
    top_p, top_i = lax.top_k(jax.nn.softmax(e_logit, axis=-1), TOP_K)
    gates = g_w * top_p / jnp.sum(top_p, axis=-1, keepdims=True)
    flat_e = (g_idx[:, None] * epg + top_i).reshape(-1).astype(jnp.int32)
    n_assign = n_tok * TOP_K
    order = jnp.argsort(flat_e)
    rank = jnp.argsort(order).astype(jnp.int32)
    counts = jnp.sum(flat_e[:, None] == jnp.arange(n_experts, dtype=jnp.int32)[None, :], axis=0, dtype=jnp.int32)
    starts = jnp.cumsum(counts) - counts
    padded = (counts + MOE_BLOCK - 1) // MOE_BLOCK * MOE_BLOCK
    pad_end = jnp.cumsum(padded)
    pad_start = pad_end - padded
    dest = (pad_start[flat_e] + rank - starts[flat_e]).astype(jnp.int32).reshape(n_tok, TOP_K)
    n_blocks = -(-n_assign // MOE_BLOCK) + n_experts
    block_start = jnp.arange(n_blocks, dtype=jnp.int32) * MOE_BLOCK
    block_e = jnp.minimum(jnp.sum(pad_end[None, :] <= block_start[:, None], axis=1), n_experts - 1).astype(jnp.int32)
    row = jnp.arange(n_blocks * MOE_BLOCK, dtype=jnp.int32)
    row_e = jnp.repeat(block_e, MOE_BLOCK)
    src = order[jnp.clip(starts[row_e] + row - pad_start[row_e], 0, n_assign - 1)]
    x_disp = tok[src // TOP_K]
    n_valid = (pad_end[-1:] // MOE_BLOCK).astype(jnp.int32)
    y_disp = moe_experts(x_disp, block_e, n_valid, w_up, w_down, layer)
    for kth in range(TOP_K):
        h = h + gates[:, kth:kth + 1] * y_disp[dest[:, kth]]
    return h


def kernel(x_prompt, x_sample, cache_k, cache_v, state_wkv, state_shift, page_table, meta_tokens, ln_mix, ln_ffn, sb_w_qkv, sb_q_norm, sb_k_norm, sb_logit_bias, sb_w_o, rw_mu, rw_w_r, rw_w_k, rw_w_v, rw_w_o, rw_w0, rw_w_la, rw_w_lb, rw_a0, rw_a_la, rw_a_lb, rw_g_la, rw_g_lb, rw_k_k, rw_k_a, rw_r_k, rw_ln_w, rw_ln_b, moe_w_group, moe_b_group, moe_w_expert, moe_b_expert, moe_w_up, moe_w_down):
    n_b, seq, d = x_prompt.shape
    n_s, n_new, _ = x_sample.shape
    n_t = N_META + seq
    n_p = n_b * n_t
    sb_heads = d // SB_DH
    rw_heads = d // RW_DH
    depth = ln_mix.shape[0]
    moe = (moe_w_group, moe_b_group, moe_w_expert, moe_b_expert, moe_w_up, moe_w_down)

    meta = jnp.broadcast_to(meta_tokens[None], (n_b, N_META, d))
    h = jnp.concatenate([jnp.concatenate([meta, x_prompt], axis=1).reshape(n_p, d), x_sample.reshape(n_s * n_new, d)],
                        axis=0)
    pool = cache_k.shape[1]
    cache_k2 = cache_k.reshape(cache_k.shape[0], pool, PAGE * sb_heads, SB_DH)
    cache_v2 = cache_v.reshape(cache_v.shape[0], pool, PAGE * sb_heads, SB_DH)

    k_p, v_p, k_s, v_s, wkv_p, wkv_s, sh_p, sh_s = [], [], [], [], [], [], [], []
    for i in range(depth):
        li = i // 2
        xn = rmsnorm(h, ln_mix[i])
        if i % 2 == 0:
            qkv = matmul_qkv(xn, sb_w_qkv[li], sb_q_norm[li], sb_k_norm[li])
            kx, vx = qkv[n_p:, d:2 * d], qkv[n_p:, 2 * d:]
            k3, v3 = kv_head_rows(qkv, 0, n_p)
            k_p.append(k3.reshape(n_b, n_t, sb_heads, SB_DH))
            v_p.append(v3.reshape(n_b, n_t, sb_heads, SB_DH))
            k3, v3 = kv_head_rows(qkv, n_p, n_s * n_new)
            k_s.append(k3.reshape(n_s, n_new, sb_heads, SB_DH))
            v_s.append(v3.reshape(n_s, n_new, sb_heads, SB_DH))
            o_p = sb_prompt_attn(qkv, sb_logit_bias[li], n_b, n_t, nh=min(4, sb_heads))
            q_hq = qkv[n_p:, :d].reshape(n_s, n_new, sb_heads, SB_DH).transpose(0, 2, 1, 3)
            o_s = sb_sample_attn(q_hq.reshape(n_s, sb_heads * n_new, SB_DH), kx.reshape(n_s, n_new, d),
                                 vx.reshape(n_s, n_new, d), cache_k2, cache_v2, li, page_table,
                                 sb_logit_bias[li], pp=math.gcd(page_table.shape[1], 4))
            o_s = o_s.reshape(n_s, sb_heads, n_new, SB_DH).transpose(0, 2, 1, 3).reshape(n_s * n_new, d)
            h = matmul_residual(jnp.concatenate([o_p, o_s], axis=0), sb_w_o[li], h)
        else:
            xp3 = xn[:n_p].reshape(n_b, n_t, d)
            xs3 = xn[n_p:].reshape(n_s, n_new, d)
            sh_p.append(xp3[:, -1])
            sh_s.append(xs3[:, -1])
            x_prev = jnp.concatenate(
                [jnp.concatenate([jnp.zeros((n_b, 1, d), F32), xp3[:, :-1]], axis=1).reshape(n_p, d),
                 jnp.concatenate([state_shift[li][:, None], xs3[:, :-1]], axis=1).reshape(n_s * n_new, d)], axis=0)
            mu = rw_mu[li]
            r = matmul_mix(xn, x_prev, mu[0], rw_w_r[li])
            k = matmul_mix(xn, x_prev, mu[2], rw_w_k[li])
            v = matmul_mix(xn, x_prev, mu[3], rw_w_v[li])
            wl = matmul(matmul_mix(xn, x_prev, mu[1], rw_w_la[li], act="tanh"), rw_w_lb[li])
            al = matmul(matmul_mix(xn, x_prev, mu[4], rw_a_la[li]), rw_a_lb[li])
            g = matmul(matmul_mix(xn, x_prev, mu[5], rw_g_la[li], act="sigmoid"), rw_g_lb[li])
            dec, kk, b, km = rw_prep(k, wl, al, rw_w0[li], rw_a0[li], rw_k_k[li], rw_k_a[li])
            y_p, s_p = wkv_scan(r, dec, km, v, kk, b, jnp.zeros((n_b, rw_heads, RW_DH, RW_DH), F32), 0, n_t)
            y_s, s_s = wkv_scan(r, dec, km, v, kk, b, state_wkv[li], n_p, n_new)
            wkv_p.append(s_p)
            wkv_s.append(s_s)
            z = rw_post(jnp.concatenate([y_p, y_s], axis=0), r, km, v, g, rw_r_k[li].reshape(-1), rw_ln_w[li],
                        rw_ln_b[li])
            h = matmul_residual(z, rw_w_o[li], h)
        h = hier_moe(h, rmsnorm(h, ln_ffn[i]), i, *moe)
    return (h[:n_p].reshape(n_b, n_t, d)[:, N_META:], h[n_p:].reshape(n_s, n_new, d),
            jnp.stack(k_p), jnp.stack(v_p), jnp.stack(k_s), jnp.stack(v_s),
            jnp.stack(wkv_p), jnp.stack(wkv_s), jnp.stack(sh_p), jnp.stack(sh_s))
```

```python
import functools
import math

import jax
import jax.numpy as jnp
from jax import lax
from jax.experimental import pallas as pl
from jax.experimental.pallas import tpu as pltpu

N_META = 16
RMS_EPS = 1e-6
GN_EPS = 64e-5
SB_DH = 128
QB = 128
PAGE = 128
RW_DH = 64
TOP_K = 2
MOE_BLOCK = 128

F32 = jnp.float32
BF16 = jnp.bfloat16
NT_DIMS = (((1,), (1,)), ((), ()))

VMEM_LIMIT = 56 * 1024 * 1024
ROW_TILE = 512
COL_TILE = 1024


def _params(n_axes):
    return pltpu.CompilerParams(dimension_semantics=("arbitrary",) * n_axes, vmem_limit_bytes=VMEM_LIMIT)


def _divisor_tile(n, cap, mult):
    best = None
    for t in range(mult, min(n, cap) + 1, mult):
        if n % t == 0:
            best = t
    assert best is not None, (n, cap, mult)
    return best


def _split_dot(x, w2):
    hi = x.astype(BF16)
    lo = (x - hi.astype(F32)).astype(BF16)
    return jnp.dot(jnp.concatenate([hi, lo], axis=1), w2, preferred_element_type=F32)


def _head_sum_weights(dh):
    r = lax.broadcasted_iota(jnp.int32, (256, 128), 0)
    c = lax.broadcasted_iota(jnp.int32, (256, 128), 1)
    return (((r % 128) // dh) == (c // dh)).astype(BF16)


def _head_sum(x, w2):
    return jnp.concatenate([_split_dot(x[:, c:c + 128], w2) for c in range(0, x.shape[1], 128)], axis=1)


def _rmsnorm_kernel(x_ref, g_ref, o_ref):
    x = x_ref[...]
    o_ref[...] = x * lax.rsqrt(jnp.mean(x * x, axis=-1, keepdims=True) + RMS_EPS) * g_ref[...]


def rmsnorm(x, gain):
    m, d = x.shape
    tm = min(ROW_TILE, m)
    return pl.pallas_call(
        _rmsnorm_kernel,
        grid=(pl.cdiv(m, tm),),
        in_specs=[pl.BlockSpec((tm, d), lambda i: (i, 0)), pl.BlockSpec((1, d), lambda i: (0, 0))],
        out_specs=pl.BlockSpec((tm, d), lambda i: (i, 0)),
        out_shape=jax.ShapeDtypeStruct((m, d), F32),
        compiler_params=_params(1),
        name="rmsnorm",
    )(x, gain.reshape(1, d))


def _bdot(x, w_ref, w_s):
    @pl.when(pl.program_id(1) == 0)
    def _():
        w_s[...] = w_ref[...].astype(BF16)
    return jnp.dot(x.astype(BF16), w_s[...], preferred_element_type=F32)


def _mm_kernel(x_ref, w_ref, o_ref, w_s):
    o_ref[...] = _bdot(x_ref[...], w_ref, w_s)


def _mm_res_kernel(x_ref, w_ref, r_ref, o_ref, w_s):
    o_ref[...] = r_ref[...] + _bdot(x_ref[...], w_ref, w_s)


def _mm_mix_kernel(x_ref, xp_ref, mu_ref, w_ref, o_ref, w_s, *, act):
    x = x_ref[...]
    y = _bdot(x + (xp_ref[...] - x) * mu_ref[...], w_ref, w_s)
    if act == "tanh":
        y = jnp.tanh(y)
    elif act == "sigmoid":
        y = jax.nn.sigmoid(y)
    o_ref[...] = y


def _mm_qkv_kernel(x_ref, w_ref, gq_ref, gk_ref, o_ref, w_s, *, n_q_tiles):
    j = pl.program_id(0)
    y = _bdot(x_ref[...], w_ref, w_s)
    gain = jnp.where(j < n_q_tiles, gq_ref[...], gk_ref[...])
    cols = []
    for c in range(0, y.shape[1], SB_DH):
        yh = y[:, c:c + SB_DH]
        cols.append(yh * lax.rsqrt(jnp.mean(yh * yh, axis=-1, keepdims=True) + RMS_EPS) * gain)
    o_ref[...] = jnp.where(j < 2 * n_q_tiles, jnp.concatenate(cols, axis=1), y)


def _matmul_call(kern, x_like, w, extra_in, extra_specs, name, tn=None):
    m, k = x_like[0].shape
    n = w.shape[1]
    tm, tn = min(ROW_TILE, m), tn or min(COL_TILE, n)
    xs = pl.BlockSpec((tm, k), lambda j, i: (i, 0))
    return pl.pallas_call(
        kern,
        grid=(pl.cdiv(n, tn), pl.cdiv(m, tm)),
        in_specs=[xs] * len(x_like) + extra_specs(tm, tn, k),
        out_specs=pl.BlockSpec((tm, tn), lambda j, i: (i, j)),
        out_shape=jax.ShapeDtypeStruct((m, n), F32),
        scratch_shapes=[pltpu.VMEM((k, tn), BF16)],
        compiler_params=_params(2),
        name=name,
    )(*x_like, *extra_in(w))


def _w_spec(tn, k):
    return pl.BlockSpec((k, tn), lambda j, i: (0, j))


def matmul(x, w):
    return _matmul_call(_mm_kernel, [x], w, lambda w: [w], lambda tm, tn, k: [_w_spec(tn, k)], "dense_matmul")


def matmul_residual(x, w, res):
    return _matmul_call(_mm_res_kernel, [x], w, lambda w: [w, res],
                        lambda tm, tn, k: [_w_spec(tn, k), pl.BlockSpec((tm, tn), lambda j, i: (i, j))],
                        "dense_matmul_residual")


def matmul_mix(x, x_prev, mu, w, act=None):
    return _matmul_call(functools.partial(_mm_mix_kernel, act=act), [x, x_prev], w,
                        lambda w: [mu.reshape(1, -1), w],
                        lambda tm, tn, k: [pl.BlockSpec((1, k), lambda j, i: (0, 0)), _w_spec(tn, k)],
                        "token_shift_matmul")


def _lora_in_kernel(x_ref, xp_ref, mu_ref, ww_ref, wa_ref, wg_ref, ow_ref, oa_ref, og_ref):
    x = x_ref[...]
    dx = xp_ref[...] - x
    proj = lambda j, w_ref: jnp.dot((x + dx * mu_ref[j:j + 1, :]).astype(BF16), w_ref[...].astype(BF16),
                                    preferred_element_type=F32)
    ow_ref[...] = jnp.tanh(proj(0, ww_ref))
    oa_ref[...] = proj(1, wa_ref)
    og_ref[...] = jax.nn.sigmoid(proj(2, wg_ref))


def lora_in(x, x_prev, mus, w_la, a_la, g_la):
    m, d = x.shape
    tm = min(ROW_TILE, m)
    tok = pl.BlockSpec((tm, d), lambda i: (i, 0))
    whole = lambda a: pl.BlockSpec(a.shape, lambda i: (0, 0))
    ws = (w_la, a_la, g_la)
    return pl.pallas_call(
        _lora_in_kernel,
        grid=(pl.cdiv(m, tm),),
        in_specs=[tok, tok, whole(mus)] + [whole(w) for w in ws],
        out_specs=[pl.BlockSpec((tm, w.shape[1]), lambda i: (i, 0)) for w in ws],
        out_shape=[jax.ShapeDtypeStruct((m, w.shape[1]), F32) for w in ws],
        compiler_params=_params(1),
        name="lora_in",
    )(x, x_prev, mus, *ws)


def matmul_qkv(x, w, g_q, g_k):
    d = x.shape[1]
    tn = min(COL_TILE, d)
    assert d % tn == 0 and tn % SB_DH == 0
    gspec = lambda tm, tn_, k: [_w_spec(tn_, k), pl.BlockSpec((1, SB_DH), lambda j, i: (0, 0)),
                                pl.BlockSpec((1, SB_DH), lambda j, i: (0, 0))]
    return _matmul_call(functools.partial(_mm_qkv_kernel, n_q_tiles=d // tn), [x], w,
                        lambda w: [w, g_q.reshape(1, SB_DH), g_k.reshape(1, SB_DH)], gspec, "qkv_headnorm_matmul",
                        tn=tn)


def _head_rows_kernel(k_ref, v_ref, ko_ref, vo_ref, *, n_heads):
    tm = k_ref.shape[0]
    for h in range(n_heads):
        cols = slice(SB_DH * h, SB_DH * (h + 1))
        ko_ref[pl.ds(h, tm, stride=n_heads), :] = k_ref[:, cols]
        vo_ref[pl.ds(h, tm, stride=n_heads), :] = v_ref[:, cols]


def kv_head_rows(qkv, row0, n_rows):
    d = qkv.shape[1] // 3
    n_heads = d // SB_DH
    tm = _divisor_tile(math.gcd(row0, n_rows), 256, 8)
    out = jax.ShapeDtypeStruct((n_rows * n_heads, SB_DH), F32)
    ospec = pl.BlockSpec((tm * n_heads, SB_DH), lambda i: (i, 0))
    return pl.pallas_call(
        functools.partial(_head_rows_kernel, n_heads=n_heads),
        grid=(n_rows // tm,),
        in_specs=[pl.BlockSpec((tm, d), lambda i: (row0 // tm + i, 1)), pl.BlockSpec((tm, d), lambda i: (row0 // tm + i, 2))],
        out_specs=[ospec, ospec],
        out_shape=[out, out],
        compiler_params=_params(1),
        name="kv_head_rows",
    )(qkv, qkv)


def _suffix_weights():
    r = lax.broadcasted_iota(jnp.int32, (QB, QB), 0)
    c = lax.broadcasted_iota(jnp.int32, (QB, QB), 1)
    w = jnp.concatenate([(r > c).astype(BF16), jnp.ones((QB, QB), BF16)], axis=1)
    return jnp.concatenate([w, w], axis=0)


def _sb_logits(z):
    ls = -(jnp.maximum(z, 0.0) + jnp.log1p(jnp.exp(-jnp.abs(z))))
    return ls, ls + z


def _sb_scores(qbs, kbs, mask, biases, w2):
    lss, lsgs = [], []
    for qb, kb, bias in zip(qbs, kbs, biases):
        z = lax.dot_general(qb, kb, NT_DIMS, preferred_element_type=F32) * (SB_DH ** -0.5) + bias
        ls, lsg = _sb_logits(z)
        lsgs.append(lsg)
        lss.append(ls if mask is None else jnp.where(mask, ls, 0.0))
    return lsgs, _split_dot(jnp.concatenate(lss, axis=0), w2)


def _sb_weigh(lsgs, res, vbs, mask, cbcs, accs):
    new_cbcs, new_accs = [], []
    for i in range(len(vbs)):
        ri = res[QB * i:QB * (i + 1)]
        w = jnp.exp(lsgs[i] + ri[:, :QB] + cbcs[i])
        if mask is not None:
            w = jnp.where(mask, w, 0.0)
        new_accs.append(accs[i] + jnp.dot(w.astype(BF16), vbs[i], preferred_element_type=F32))
        new_cbcs.append(cbcs[i] + ri[:, QB:])
    return tuple(new_cbcs), tuple(new_accs)


def _sb_tile(qbs, kbs, vbs, mask, biases, w2, cbcs, accs):
    lsgs, res = _sb_scores(qbs, kbs, mask, biases, w2)
    return _sb_weigh(lsgs, res, vbs, mask, cbcs, accs)


def _sbp_kernel(bias_ref, q_ref, k_ref, v_ref, o_ref, kpad, vpad, *, n_blk, nh):
    h0 = pl.program_id(1) * nh
    biases = [bias_ref[h0 + i] for i in range(nh)]
    hs = [slice(SB_DH * i, SB_DH * (i + 1)) for i in range(nh)]
    w2 = _suffix_weights()
    r = lax.broadcasted_iota(jnp.int32, (QB, QB), 0)
    c = lax.broadcasted_iota(jnp.int32, (QB, QB), 1)
    diag_mask = c < r
    meta_mask = c < N_META
    zeros = tuple(jnp.zeros((QB, QB), F32) for _ in range(nh))

    kpad[...] = jnp.zeros_like(kpad)
    vpad[...] = jnp.zeros_like(vpad)
    kpad[0:N_META, :] = k_ref[0:N_META, :].astype(BF16)
    vpad[0:N_META, :] = v_ref[0:N_META, :].astype(BF16)
    rows = lambda ref, r0: [ref[pl.ds(r0, QB), s].astype(BF16) for s in hs]
    kms = [kpad[:, s] for s in hs]
    vms = [vpad[:, s] for s in hs]

    _, accs = _sb_tile(rows(q_ref, 0), kms, vms, diag_mask & meta_mask, biases, w2, zeros, zeros)
    for i in range(nh):
        o_ref[0:N_META, hs[i]] = accs[i][0:N_META]

    def q_block(i, carry):
        q0 = pl.multiple_of(N_META + QB * i, 8)
        qbs = rows(q_ref, q0)
        ca = _sb_tile(qbs, rows(k_ref, q0), rows(v_ref, q0), diag_mask, biases, w2, zeros, zeros)

        def k_pair(j, ca):
            m0 = i - 1 - 2 * j
            m1 = m0 - 1
            k0 = pl.multiple_of(N_META + QB * m0, 8)
            k1 = pl.multiple_of(N_META + QB * jnp.maximum(m1, 0), 8)
            real = jnp.full((QB, QB), m1, jnp.int32) >= 0
            s0 = _sb_scores(qbs, rows(k_ref, k0), None, biases, w2)
            s1 = _sb_scores(qbs, rows(k_ref, k1), real, biases, w2)
            ca = _sb_weigh(*s0, rows(v_ref, k0), None, *ca)
            return _sb_weigh(*s1, rows(v_ref, k1), real, *ca)

        ca = lax.fori_loop(0, (i + 1) // 2, k_pair, ca)
        _, accs = _sb_tile(qbs, kms, vms, meta_mask, biases, w2, *ca)
        for hh in range(nh):
            o_ref[pl.ds(q0, QB), hs[hh]] = accs[hh]
        return carry

    lax.fori_loop(0, n_blk, q_block, 0)


def sb_prompt_attn(qkv, bias, n_b, n_t, nh):
    d = qkv.shape[1] // 3
    hb = d // (SB_DH * nh)
    blk = lambda off: pl.BlockSpec((n_t, SB_DH * nh), lambda b, h, off=off: (b, off + h))
    return pl.pallas_call(
        functools.partial(_sbp_kernel, n_blk=(n_t - N_META) // QB, nh=nh),
        grid=(n_b, hb),
        in_specs=[pl.BlockSpec(memory_space=pltpu.SMEM), blk(0), blk(hb), blk(2 * hb)],
        out_specs=pl.BlockSpec((n_t, SB_DH * nh), lambda b, h: (b, h)),
        out_shape=jax.ShapeDtypeStruct((n_b * n_t, d), F32),
        scratch_shapes=[pltpu.VMEM((QB, SB_DH * nh), BF16), pltpu.VMEM((QB, SB_DH * nh), BF16)],
        compiler_params=_params(2),
        name="sb_prompt_attn",
    )(bias, qkv, qkv, qkv)


def _sbs_kernel(pt_ref, bias_ref, q_ref, kn_ref, vn_ref, *rest, n_heads, n_new, pp):
    kp_refs, vp_refs = rest[:pp], rest[pp:2 * pp]
    o_ref, cbc_scr, acc_scr, pad_scr = rest[2 * pp:]
    j = pl.program_id(1)
    w2 = _suffix_weights()
    hs = [slice(SB_DH * h, SB_DH * (h + 1)) for h in range(n_heads)]
    r = lax.broadcasted_iota(jnp.int32, (n_new * n_heads, PAGE), 0)
    c = lax.broadcasted_iota(jnp.int32, (n_new * n_heads, PAGE), 1)
    bias_col = jnp.concatenate([jnp.full((n_new, 1), bias_ref[h], F32) for h in range(n_heads)], axis=0)
    q = q_ref[0]
    qh = [q[n_new * h:n_new * (h + 1)].astype(BF16) for h in range(n_heads)]
    head_rows = lambda ref, h: ref[0, 0, pl.ds(h, PAGE, stride=n_heads), :].astype(BF16)

    def scores(k_at, mask):
        z = jnp.concatenate([lax.dot_general(qh[h], k_at(h), NT_DIMS, preferred_element_type=F32)
                             for h in range(n_heads)], axis=0) * (SB_DH ** -0.5) + bias_col
        ls, lsg = _sb_logits(z)
        if mask is not None:
            ls = jnp.where(mask, ls, 0.0)
        return lsg, _split_dot(ls, w2)

    def weigh(lsg, res, v_at, mask, cbc, acc):
        w = jnp.exp(lsg + res[:, :PAGE] + cbc)
        if mask is not None:
            w = jnp.where(mask, w, 0.0)
        pv = jnp.concatenate(
            [jnp.dot(w[n_new * h:n_new * (h + 1)].astype(BF16), v_at(h), preferred_element_type=F32)
             for h in range(n_heads)], axis=0)
        return cbc + res[:, PAGE:], acc + pv

    @pl.when(j == 0)
    def _():
        pad_scr[...] = jnp.zeros_like(pad_scr)
        pad_scr[0, 0:n_new, :] = kn_ref[0].astype(BF16)
        pad_scr[1, 0:n_new, :] = vn_ref[0].astype(BF16)
        mask = c < (r % n_new)
        zeros = jnp.zeros((n_new * n_heads, PAGE), F32)
        lsg, res = scores(lambda h: pad_scr[0, :, hs[h]], mask)
        cbc, acc = weigh(lsg, res, lambda h: pad_scr[1, :, hs[h]], mask, zeros, zeros)
        cbc_scr[...] = cbc
        acc_scr[...] = acc

    sc = [scores(functools.partial(head_rows, kp), None) for kp in kp_refs]
    cbc, acc = cbc_scr[...], acc_scr[...]
    for (lsg, res), vp in zip(sc, vp_refs):
        cbc, acc = weigh(lsg, res, functools.partial(head_rows, vp), None, cbc, acc)
    cbc_scr[...] = cbc
    acc_scr[...] = acc

    @pl.when(j == pl.num_programs(1) - 1)
    def _():
        o_ref[0] = acc


def sb_sample_attn(q_hq, k_new, v_new, cache_k, cache_v, layer, page_table, bias, pp):
    n_s, n_pages = page_table.shape
    n_new, d = k_new.shape[1:]
    n_heads = d // SB_DH
    assert n_pages % pp == 0
    page_specs = [pl.BlockSpec((1, 1, PAGE * n_heads, SB_DH),
                               lambda s, j, pt, b, i=i: (layer, pt[s, n_pages - 1 - (j * pp + i)], 0, 0))
                  for i in range(pp)]
    seq3 = lambda s, j, pt, b: (s, 0, 0)
    return pl.pallas_call(
        functools.partial(_sbs_kernel, n_heads=n_heads, n_new=n_new, pp=pp),
        grid_spec=pltpu.PrefetchScalarGridSpec(
            num_scalar_prefetch=2,
            grid=(n_s, n_pages // pp),
            in_specs=[pl.BlockSpec((1, n_new * n_heads, SB_DH), seq3),
                      pl.BlockSpec((1, n_new, d), seq3),
                      pl.BlockSpec((1, n_new, d), seq3)] + page_specs + page_specs,
            out_specs=pl.BlockSpec((1, n_new * n_heads, SB_DH), seq3),
            scratch_shapes=[pltpu.VMEM((n_new * n_heads, PAGE), F32), pltpu.VMEM((n_new * n_heads, SB_DH), F32),
                            pltpu.VMEM((2, PAGE, d), BF16)]),
        out_shape=jax.ShapeDtypeStruct((n_s, n_new * n_heads, SB_DH), F32),
        compiler_params=_params(2),
        name="sb_sample_attn",
    )(page_table, bias, q_hq, k_new, v_new, *([cache_k] * pp), *([cache_v] * pp))


def _rw_prep_kernel(k_ref, wl_ref, al_ref, w0_ref, a0_ref, kk_ref_, ka_ref, dec_o, kk_o, b_o, km_o):
    k = k_ref[...]
    w2 = _head_sum_weights(RW_DH)
    wpre = w0_ref[...] + wl_ref[...]
    sp = jnp.maximum(-wpre, 0.0) + jnp.log1p(jnp.exp(-jnp.abs(wpre)))
    dec_o[...] = jnp.exp(-jnp.exp(-sp - 0.5))
    a = jax.nn.sigmoid(a0_ref[...] + al_ref[...])
    kk = k * kk_ref_[...]
    kk = kk / jnp.maximum(jnp.sqrt(_head_sum(kk * kk, w2)), 1e-12)
    kk_o[...] = kk
    b_o[...] = kk * a
    km_o[...] = k * (1.0 + (a - 1.0) * ka_ref[...])


def rw_prep(k, wl, al, w0, a0, k_k, k_a):
    m, d = k.shape
    tm = min(256, m)
    tok = pl.BlockSpec((tm, d), lambda i: (i, 0))
    row = pl.BlockSpec((1, d), lambda i: (0, 0))
    out = jax.ShapeDtypeStruct((m, d), F32)
    return pl.pallas_call(
        _rw_prep_kernel,
        grid=(pl.cdiv(m, tm),),
        in_specs=[tok] * 3 + [row] * 4,
        out_specs=[tok] * 4,
        out_shape=[out] * 4,
        compiler_params=_params(1),
        name="rwkv_prep",
    )(k, wl, al, *(p.reshape(1, d) for p in (w0, a0, k_k, k_a)))


def _wkv_kernel(r_ref, d_ref, k_ref, v_ref, kk_ref, b_ref, s0_ref, y_ref, sT_ref, s_scr, *, n_pairs, tc):
    c = pl.program_id(1)

    @pl.when(c == 0)
    def _():
        for p in range(n_pairs):
            s_scr[RW_DH * p:RW_DH * (p + 1), :] = jnp.concatenate([s0_ref[0, 2 * p], s0_ref[0, 2 * p + 1]], axis=-1)

    row = lax.broadcasted_iota(jnp.int32, (RW_DH, 2 * RW_DH), 0)
    col = lax.broadcasted_iota(jnp.int32, (RW_DH, 2 * RW_DH), 1)
    sel = jnp.tile(row == (col % RW_DH), (n_pairs, 1))
    w2 = _head_sum_weights(RW_DH)

    def group(g, carry):
        t0 = pl.multiple_of(g * 8, 8)
        ld = lambda ref: ref[pl.ds(t0, 8), :]
        r8, d8, k8, v8, kk8, b8 = ld(r_ref), ld(d_ref), ld(k_ref), ld(v_ref), ld(kk_ref), ld(b_ref)
        s = s_scr[...]
        vT = [v8[:, 2 * RW_DH * p:2 * RW_DH * (p + 1)].T for p in range(n_pairs)]
        ys = []
        for j in range(8):
            def bc(x8):
                return jnp.concatenate(
                    [jnp.broadcast_to(x8[j:j + 1, 2 * RW_DH * p:2 * RW_DH * (p + 1)], (RW_DH, 2 * RW_DH))
                     for p in range(n_pairs)], axis=0)
            sa = _split_dot(s * bc(kk8), w2)
            vbc = jnp.concatenate(
                [jnp.concatenate([jnp.broadcast_to(vT[p][RW_DH * hh:RW_DH * (hh + 1), j:j + 1], (RW_DH, RW_DH))
                                  for hh in range(2)], axis=1) for p in range(n_pairs)], axis=0)
            s = s * bc(d8) - sa * bc(b8) + vbc * bc(k8)
            ybc = _split_dot(s * bc(r8), w2)
            ys.append(jnp.sum(jnp.where(sel, ybc, 0.0).reshape(n_pairs, RW_DH, 2 * RW_DH), axis=1))
        s_scr[...] = s
        y_ref[pl.ds(t0, 8), :] = jnp.concatenate(
            [jnp.concatenate([ys[j][p:p + 1] for j in range(8)], axis=0) for p in range(n_pairs)], axis=1)
        return carry

    lax.fori_loop(0, tc // 8, group, 0)

    @pl.when(c == pl.num_programs(1) - 1)
    def _():
        for p in range(n_pairs):
            s = s_scr[RW_DH * p:RW_DH * (p + 1), :]
            sT_ref[0, 2 * p] = s[:, :RW_DH]
            sT_ref[0, 2 * p + 1] = s[:, RW_DH:]


def wkv_scan(r, dec, k, v, kk, b, s0, row0, n_t):
    d = r.shape[1]
    n_b, n_h = s0.shape[:2]
    tc = _divisor_tile(n_t, 48, 8)
    assert row0 % tc == 0
    nc = n_t // tc
    tok = pl.BlockSpec((tc, d), lambda i, c: (row0 // tc + i * nc + c, 0))
    st = pl.BlockSpec((1, n_h, RW_DH, RW_DH), lambda i, c: (i, 0, 0, 0))
    return pl.pallas_call(
        functools.partial(_wkv_kernel, n_pairs=n_h // 2, tc=tc),
        grid=(n_b, nc),
        in_specs=[tok] * 6 + [st],
        out_specs=[pl.BlockSpec((tc, d), lambda i, c: (i * nc + c, 0)), st],
        out_shape=[jax.ShapeDtypeStruct((n_b * n_t, d), F32), jax.ShapeDtypeStruct(s0.shape, F32)],
        scratch_shapes=[pltpu.VMEM((n_h // 2 * RW_DH, 2 * RW_DH), F32)],
        compiler_params=_params(2),
        name="wkv7_scan",
    )(r, dec, k, v, kk, b, s0)


def _rw_post_kernel(y_ref, r_ref, km_ref, v_ref, g_ref, rk_ref, lw_ref, lb_ref, o_ref):
    w2 = _head_sum_weights(RW_DH)
    y = y_ref[...]
    mean = _head_sum(y, w2) * (1.0 / RW_DH)
    yc = y - mean
    var = _head_sum(yc * yc, w2) * (1.0 / RW_DH)
    yn = yc * lax.rsqrt(var + GN_EPS) * lw_ref[...] + lb_ref[...]
    bonus = _head_sum(r_ref[...] * km_ref[...] * rk_ref[...], w2) * v_ref[...]
    o_ref[...] = (yn + bonus) * g_ref[...]


def rw_post(y, r, km, v, g, r_k, ln_w, ln_b):
    m, d = y.shape
    tm = min(256, m)
    tok = pl.BlockSpec((tm, d), lambda i: (i, 0))
    row = pl.BlockSpec((1, d), lambda i: (0, 0))
    return pl.pallas_call(
        _rw_post_kernel,
        grid=(pl.cdiv(m, tm),),
        in_specs=[tok] * 5 + [row] * 3,
        out_specs=tok,
        out_shape=jax.ShapeDtypeStruct((m, d), F32),
        compiler_params=_params(1),
        name="rwkv_post",
    )(y, r, km, v, g, *(p.reshape(1, d) for p in (r_k, ln_w, ln_b)))


def _moe_up_kernel(be_ref, nv_ref, x_ref, wg_ref, wu_ref, o_ref, wg_s, wu_s):
    i = pl.program_id(1)
    e = be_ref[i]
    prev = be_ref[jnp.maximum(i - 1, 0)]

    @pl.when((i == 0) | (e != prev))
    def _():
        wg_s[...] = wg_ref[0, 0].astype(BF16)
        wu_s[...] = wu_ref[0, 0].astype(BF16)

    @pl.when(i < nv_ref[0])
    def _():
        x = x_ref[...].astype(BF16)
        gate = jnp.dot(x, wg_s[...], preferred_element_type=F32)
        up = jnp.dot(x, wu_s[...], preferred_element_type=F32)
        o_ref[...] = gate * jax.nn.sigmoid(gate) * up

    @pl.when(i >= nv_ref[0])
    def _():
        o_ref[...] = jnp.zeros_like(o_ref)


def _moe_down_kernel(be_ref, nv_ref, h_ref, w_ref, o_ref, w_s):
    i = pl.program_id(1)
    e = be_ref[i]
    prev = be_ref[jnp.maximum(i - 1, 0)]

    @pl.when((i == 0) | (e != prev))
    def _():
        w_s[...] = w_ref[0, 0].astype(BF16)

    @pl.when(i < nv_ref[0])
    def _():
        o_ref[...] = jnp.dot(h_ref[...].astype(BF16), w_s[...], preferred_element_type=F32)

    @pl.when(i >= nv_ref[0])
    def _():
        o_ref[...] = jnp.zeros_like(o_ref)


def moe_experts(x_disp, block_e, n_valid, w_up, w_down, layer):
    n_rows, d = x_disp.shape
    n_blocks = n_rows // MOE_BLOCK
    f = w_down.shape[2]
    th = min(COL_TILE, f)
    nf = f // th
    hid = pl.pallas_call(
        _moe_up_kernel,
        grid_spec=pltpu.PrefetchScalarGridSpec(
            num_scalar_prefetch=2,
            grid=(nf, n_blocks),
            in_specs=[pl.BlockSpec((MOE_BLOCK, d), lambda j, i, be, nv: (i, 0)),
                      pl.BlockSpec((1, 1, d, th), lambda j, i, be, nv: (layer, be[i], 0, j)),
                      pl.BlockSpec((1, 1, d, th), lambda j, i, be, nv: (layer, be[i], 0, nf + j))],
            out_specs=pl.BlockSpec((MOE_BLOCK, th), lambda j, i, be, nv: (i, j)),
            scratch_shapes=[pltpu.VMEM((d, th), BF16), pltpu.VMEM((d, th), BF16)]),
        out_shape=jax.ShapeDtypeStruct((n_rows, f), F32),
        compiler_params=_params(2),
        name="moe_up_gate",
    )(block_e, n_valid, x_disp, w_up, w_up)
    tn = min(2 * COL_TILE, d)
    return pl.pallas_call(
        _moe_down_kernel,
        grid_spec=pltpu.PrefetchScalarGridSpec(
            num_scalar_prefetch=2,
            grid=(d // tn, n_blocks),
            in_specs=[pl.BlockSpec((MOE_BLOCK, f), lambda j, i, be, nv: (i, 0)),
                      pl.BlockSpec((1, 1, f, tn), lambda j, i, be, nv: (layer, be[i], 0, j))],
            out_specs=pl.BlockSpec((MOE_BLOCK, tn), lambda j, i, be, nv: (i, j)),
            scratch_shapes=[pltpu.VMEM((f, tn), BF16)]),
        out_shape=jax.ShapeDtypeStruct((n_rows, d), F32),
        compiler_params=_params(2),
        name="moe_down",
    )(block_e, n_valid, hid, w_down)


def hier_moe(h, tok, layer, w_group, b_group, w_expert, b_expert, w_up, w_down):
    n_tok, d = tok.shape
    n_groups = w_group.shape[-1]
    n_experts = w_expert.shape[-1]
    epg = n_experts // n_groups
    logits = matmul(tok, jnp.concatenate([w_group[layer], w_expert[layer]], axis=1))
    g_logit = logits[:, :n_groups] + b_group[layer]
    g_prob = jax.nn.softmax(g_logit, axis=-1)
    g_idx = jnp.argmax(g_logit, axis=-1)
    g_w = jnp.take_along_axis(g_prob, g_idx[:, None], axis=1)
    e_logit = (logits[:, n_groups:] + b_expert[layer]).reshape(n_tok, n_groups, epg)
    e_logit = jnp.take_along_axis(e_logit, g_idx[:, None, None], axis=1)[:, 0]
    top_p, top_i = lax.top_k(jax.nn.softmax(e_logit, axis=-1), TOP_K)
    gates = g_w * top_p / jnp.sum(top_p, axis=-1, keepdims=True)
    flat_e = (g_idx[:, None] * epg + top_i).reshape(-1).astype(jnp.int32)
    n_assign = n_tok * TOP_K
    order = jnp.argsort(flat_e)
    rank = jnp.argsort(order).astype(jnp.int32)
    counts = jnp.sum(flat_e[:, None] == jnp.arange(n_experts, dtype=jnp.int32)[None, :], axis=0, dtype=jnp.int32)
    starts = jnp.cumsum(counts) - counts
    padded = (counts + MOE_BLOCK - 1) // MOE_BLOCK * MOE_BLOCK
    pad_end = jnp.cumsum(padded)
    pad_start = pad_end - padded
    dest = (pad_start[flat_e] + rank - starts[flat_e]).astype(jnp.int32).reshape(n_tok, TOP_K)
    n_blocks = -(-n_assign // MOE_BLOCK) + n_experts
    block_start = jnp.arange(n_blocks, dtype=jnp.int32) * MOE_BLOCK
    block_e = jnp.minimum(jnp.sum(pad_end[None, :] <= block_start[:, None], axis=1), n_experts - 1).astype(jnp.int32)
    row = jnp.arange(n_blocks * MOE_BLOCK, dtype=jnp.int32)
    row_e = jnp.repeat(block_e, MOE_BLOCK)
    src = order[jnp.clip(starts[row_e] + row - pad_start[row_e], 0, n_assign - 1)]
    x_disp = tok[src // TOP_K]
    n_valid = (pad_end[-1:] // MOE_BLOCK).astype(jnp.int32)
    y_disp = moe_experts(x_disp, block_e, n_valid, w_up, w_down, layer)
    for kth in range(TOP_K):
        h = h + gates[:, kth:kth + 1] * y_disp[dest[:, kth]]
    return h


def kernel(x_prompt, x_sample, cache_k, cache_v, state_wkv, state_shift, page_table, meta_tokens, ln_mix, ln_ffn, sb_w_qkv, sb_q_norm, sb_k_norm, sb_logit_bias, sb_w_o, rw_mu, rw_w_r, rw_w_k, rw_w_v, rw_w_o, rw_w0, rw_w_la, rw_w_lb, rw_a0, rw_a_la, rw_a_lb, rw_g_la, rw_g_lb, rw_k_k, rw_k_a, rw_r_k, rw_ln_w, rw_ln_b, moe_w_group, moe_b_group, moe_w_expert, moe_b_expert, moe_w_up, moe_w_down):
    n_b, seq, d = x_prompt.shape
    n_s, n_new, _ = x_sample.shape
    n_t = N_META + seq
    n_p = n_b * n_t
    sb_heads = d // SB_DH
    rw_heads = d // RW_DH
    depth = ln_mix.shape[0]
    moe = (moe_w_group, moe_b_group, moe_w_expert, moe_b_expert, moe_w_up, moe_w_down)

    meta = jnp.broadcast_to(meta_tokens[None], (n_b, N_META, d))
    h = jnp.concatenate([jnp.concatenate([meta, x_prompt], axis=1).reshape(n_p, d), x_sample.reshape(n_s * n_new, d)],
                        axis=0)
    pool = cache_k.shape[1]
    cache_k2 = cache_k.reshape(cache_k.shape[0], pool, PAGE * sb_heads, SB_DH)
    cache_v2 = cache_v.reshape(cache_v.shape[0], pool, PAGE * sb_heads, SB_DH)

    k_p, v_p, k_s, v_s, wkv_p, wkv_s, sh_p, sh_s = [], [], [], [], [], [], [], []
    for i in range(depth):
        li = i // 2
        xn = rmsnorm(h, ln_mix[i])
        if i % 2 == 0:
            qkv = matmul_qkv(xn, sb_w_qkv[li], sb_q_norm[li], sb_k_norm[li])
            kx, vx = qkv[n_p:, d:2 * d], qkv[n_p:, 2 * d:]
            k3, v3 = kv_head_rows(qkv, 0, n_p)
            k_p.append(k3.reshape(n_b, n_t, sb_heads, SB_DH))
            v_p.append(v3.reshape(n_b, n_t, sb_heads, SB_DH))
            k3, v3 = kv_head_rows(qkv, n_p, n_s * n_new)
            k_s.append(k3.reshape(n_s, n_new, sb_heads, SB_DH))
            v_s.append(v3.reshape(n_s, n_new, sb_heads, SB_DH))
            o_p = sb_prompt_attn(qkv, sb_logit_bias[li], n_b, n_t, nh=min(4, sb_heads))
            q_hq = qkv[n_p:, :d].reshape(n_s, n_new, sb_heads, SB_DH).transpose(0, 2, 1, 3)
            o_s = sb_sample_attn(q_hq.reshape(n_s, sb_heads * n_new, SB_DH), kx.reshape(n_s, n_new, d),
                                 vx.reshape(n_s, n_new, d), cache_k2, cache_v2, li, page_table,
                                 sb_logit_bias[li], pp=math.gcd(page_table.shape[1], 4))
            o_s = o_s.reshape(n_s, sb_heads, n_new, SB_DH).transpose(0, 2, 1, 3).reshape(n_s * n_new, d)
            h = matmul_residual(jnp.concatenate([o_p, o_s], axis=0), sb_w_o[li], h)
        else:
            xp3 = xn[:n_p].reshape(n_b, n_t, d)
            xs3 = xn[n_p:].reshape(n_s, n_new, d)
            sh_p.append(xp3[:, -1])
            sh_s.append(xs3[:, -1])
            x_prev = jnp.concatenate(
                [jnp.concatenate([jnp.zeros((n_b, 1, d), F32), xp3[:, :-1]], axis=1).reshape(n_p, d),
                 jnp.concatenate([state_shift[li][:, None], xs3[:, :-1]], axis=1).reshape(n_s * n_new, d)], axis=0)
            mu = rw_mu[li]
            r = matmul_mix(xn, x_prev, mu[0], rw_w_r[li])
            k = matmul_mix(xn, x_prev, mu[2], rw_w_k[li])
            v = matmul_mix(xn, x_prev, mu[3], rw_w_v[li])
            wl1, al1, g1 = lora_in(xn, x_prev, jnp.stack([mu[1], mu[4], mu[5]]), rw_w_la[li], rw_a_la[li],
                                   rw_g_la[li])
            wl = matmul(wl1, rw_w_lb[li])
            al = matmul(al1, rw_a_lb[li])
            g = matmul(g1, rw_g_lb[li])
            dec, kk, b, km = rw_prep(k, wl, al, rw_w0[li], rw_a0[li], rw_k_k[li], rw_k_a[li])
            y_p, s_p = wkv_scan(r, dec, km, v, kk, b, jnp.zeros((n_b, rw_heads, RW_DH, RW_DH), F32), 0, n_t)
            y_s, s_s = wkv_scan(r, dec, km, v, kk, b, state_wkv[li], n_p, n_new)
            wkv_p.append(s_p)
            wkv_s.append(s_s)
            z = rw_post(jnp.concatenate([y_p, y_s], axis=0), r, km, v, g, rw_r_k[li].reshape(-1), rw_ln_w[li],
                        rw_ln_b[li])
            h = matmul_residual(z, rw_w_o[li], h)
        h = hier_moe(h, rmsnorm(h, ln_ffn[i]), i, *moe)
    return (h[:n_p].reshape(n_b, n_t, d)[:, N_META:], h[n_p:].reshape(n_s, n_new, d),
            jnp.stack(k_p), jnp.stack(v_p), jnp.stack(k_s), jnp.stack(v_s),
            jnp.stack(wkv_p), jnp.stack(wkv_s), jnp.stack(sh_p), jnp.stack(sh_s))
```

```python
import functools
import math

import jax
import jax.numpy as jnp
from jax import lax
from jax.experimental import pallas as pl
from jax.experimental.pallas import tpu as pltpu

N_META = 16
RMS_EPS = 1e-6
GN_EPS = 64e-5
SB_DH = 128
QB = 128
PAGE = 128
RW_DH = 64
TOP_K = 2
MOE_BLOCK = 128

F32 = jnp.float32
BF16 = jnp.bfloat16
NT_DIMS = (((1,), (1,)), ((), ()))

VMEM_LIMIT = 56 * 1024 * 1024
ROW_TILE = 512
COL_TILE = 1024


def _params(n_axes):
    return pltpu.CompilerParams(dimension_semantics=("arbitrary",) * n_axes, vmem_limit_bytes=VMEM_LIMIT)


def _divisor_tile(n, cap, mult):
    best = None
    for t in range(mult, min(n, cap) + 1, mult):
        if n % t == 0:
            best = t
    assert best is not None, (n, cap, mult)
    return best


def _split_dot(x, w2):
    hi = x.astype(BF16)
    lo = (x - hi.astype(F32)).astype(BF16)
    return jnp.dot(jnp.concatenate([hi, lo], axis=1), w2, preferred_element_type=F32)


def _head_sum_weights(dh):
    r = lax.broadcasted_iota(jnp.int32, (256, 128), 0)
    c = lax.broadcasted_iota(jnp.int32, (256, 128), 1)
    return (((r % 128) // dh) == (c // dh)).astype(BF16)


def _head_sum(x, w2):
    return jnp.concatenate([_split_dot(x[:, c:c + 128], w2) for c in range(0, x.shape[1], 128)], axis=1)


def _rmsnorm_kernel(x_ref, g_ref, o_ref):
    x = x_ref[...]
    o_ref[...] = x * lax.rsqrt(jnp.mean(x * x, axis=-1, keepdims=True) + RMS_EPS) * g_ref[...]


def rmsnorm(x, gain):
    m, d = x.shape
    tm = min(ROW_TILE, m)
    return pl.pallas_call(
        _rmsnorm_kernel,
        grid=(pl.cdiv(m, tm),),
        in_specs=[pl.BlockSpec((tm, d), lambda i: (i, 0)), pl.BlockSpec((1, d), lambda i: (0, 0))],
        out_specs=pl.BlockSpec((tm, d), lambda i: (i, 0)),
        out_shape=jax.ShapeDtypeStruct((m, d), F32),
        compiler_params=_params(1),
        name="rmsnorm",
    )(x, gain.reshape(1, d))


def _bdot(x, w_ref, w_s):
    @pl.when(pl.program_id(1) == 0)
    def _():
        w_s[...] = w_ref[...].astype(BF16)
    return jnp.dot(x.astype(BF16), w_s[...], preferred_element_type=F32)


def _mm_kernel(x_ref, w_ref, o_ref, w_s):
    o_ref[...] = _bdot(x_ref[...], w_ref, w_s)


def _mm_res_kernel(x_ref, w_ref, r_ref, o_ref, w_s):
    o_ref[...] = r_ref[...] + _bdot(x_ref[...], w_ref, w_s)


def _mm_mix_kernel(x_ref, xp_ref, mu_ref, w_ref, o_ref, w_s, *, act):
    x = x_ref[...]
    y = _bdot(x + (xp_ref[...] - x) * mu_ref[...], w_ref, w_s)
    if act == "tanh":
        y = jnp.tanh(y)
    elif act == "sigmoid":
        y = jax.nn.sigmoid(y)
    o_ref[...] = y


def _mm_qkv_kernel(x_ref, w_ref, gq_ref, gk_ref, o_ref, w_s, *, n_q_tiles):
    j = pl.program_id(0)
    y = _bdot(x_ref[...], w_ref, w_s)
    gain = jnp.where(j < n_q_tiles, gq_ref[...], gk_ref[...])
    cols = []
    for c in range(0, y.shape[1], SB_DH):
        yh = y[:, c:c + SB_DH]
        cols.append(yh * lax.rsqrt(jnp.mean(yh * yh, axis=-1, keepdims=True) + RMS_EPS) * gain)
    o_ref[...] = jnp.where(j < 2 * n_q_tiles, jnp.concatenate(cols, axis=1), y)


def _matmul_call(kern, x_like, w, extra_in, extra_specs, name, tn=None):
    m, k = x_like[0].shape
    n = w.shape[1]
    tm, tn = min(ROW_TILE, m), tn or min(COL_TILE, n)
    xs = pl.BlockSpec((tm, k), lambda j, i: (i, 0))
    return pl.pallas_call(
        kern,
        grid=(pl.cdiv(n, tn), pl.cdiv(m, tm)),
        in_specs=[xs] * len(x_like) + extra_specs(tm, tn, k),
        out_specs=pl.BlockSpec((tm, tn), lambda j, i: (i, j)),
        out_shape=jax.ShapeDtypeStruct((m, n), F32),
        scratch_shapes=[pltpu.VMEM((k, tn), BF16)],
        compiler_params=_params(2),
        name=name,
    )(*x_like, *extra_in(w))


def _w_spec(tn, k):
    return pl.BlockSpec((k, tn), lambda j, i: (0, j))


def matmul(x, w):
    return _matmul_call(_mm_kernel, [x], w, lambda w: [w], lambda tm, tn, k: [_w_spec(tn, k)], "dense_matmul")


def matmul_residual(x, w, res):
    return _matmul_call(_mm_res_kernel, [x], w, lambda w: [w, res],
                        lambda tm, tn, k: [_w_spec(tn, k), pl.BlockSpec((tm, tn), lambda j, i: (i, j))],
                        "dense_matmul_residual")


def matmul_mix(x, x_prev, mu, w, act=None):
    return _matmul_call(functools.partial(_mm_mix_kernel, act=act), [x, x_prev], w,
                        lambda w: [mu.reshape(1, -1), w],
                        lambda tm, tn, k: [pl.BlockSpec((1, k), lambda j, i: (0, 0)), _w_spec(tn, k)],
                        "token_shift_matmul")


def _lora_in_kernel(x_ref, xp_ref, mu_ref, ww_ref, wa_ref, wg_ref, ow_ref, oa_ref, og_ref):
    x = x_ref[...]
    dx = xp_ref[...] - x
    proj = lambda j, w_ref: jnp.dot((x + dx * mu_ref[j:j + 1, :]).astype(BF16), w_ref[...].astype(BF16),
                                    preferred_element_type=F32)
    ow_ref[...] = jnp.tanh(proj(0, ww_ref))
    oa_ref[...] = proj(1, wa_ref)
    og_ref[...] = jax.nn.sigmoid(proj(2, wg_ref))


def lora_in(x, x_prev, mus, w_la, a_la, g_la):
    m, d = x.shape
    tm = min(ROW_TILE, m)
    tok = pl.BlockSpec((tm, d), lambda i: (i, 0))
    whole = lambda a: pl.BlockSpec(a.shape, lambda i: (0, 0))
    ws = (w_la, a_la, g_la)
    return pl.pallas_call(
        _lora_in_kernel,
        grid=(pl.cdiv(m, tm),),
        in_specs=[tok, tok, whole(mus)] + [whole(w) for w in ws],
        out_specs=[pl.BlockSpec((tm, w.shape[1]), lambda i: (i, 0)) for w in ws],
        out_shape=[jax.ShapeDtypeStruct((m, w.shape[1]), F32) for w in ws],
        compiler_params=_params(1),
        name="lora_in",
    )(x, x_prev, mus, *ws)


def matmul_qkv(x, w, g_q, g_k):
    d = x.shape[1]
    tn = min(COL_TILE, d)
    assert d % tn == 0 and tn % SB_DH == 0
    gspec = lambda tm, tn_, k: [_w_spec(tn_, k), pl.BlockSpec((1, SB_DH), lambda j, i: (0, 0)),
                                pl.BlockSpec((1, SB_DH), lambda j, i: (0, 0))]
    return _matmul_call(functools.partial(_mm_qkv_kernel, n_q_tiles=d // tn), [x], w,
                        lambda w: [w, g_q.reshape(1, SB_DH), g_k.reshape(1, SB_DH)], gspec, "qkv_headnorm_matmul",
                        tn=tn)


def _head_rows_kernel(k_ref, v_ref, ko_ref, vo_ref, *, n_heads):
    tm = k_ref.shape[0]
    for h in range(n_heads):
        cols = slice(SB_DH * h, SB_DH * (h + 1))
        ko_ref[pl.ds(h, tm, stride=n_heads), :] = k_ref[:, cols]
        vo_ref[pl.ds(h, tm, stride=n_heads), :] = v_ref[:, cols]


def kv_head_rows(qkv, row0, n_rows):
    d = qkv.shape[1] // 3
    n_heads = d // SB_DH
    tm = _divisor_tile(math.gcd(row0, n_rows), 256, 8)
    out = jax.ShapeDtypeStruct((n_rows * n_heads, SB_DH), F32)
    ospec = pl.BlockSpec((tm * n_heads, SB_DH), lambda i: (i, 0))
    return pl.pallas_call(
        functools.partial(_head_rows_kernel, n_heads=n_heads),
        grid=(n_rows // tm,),
        in_specs=[pl.BlockSpec((tm, d), lambda i: (row0 // tm + i, 1)), pl.BlockSpec((tm, d), lambda i: (row0 // tm + i, 2))],
        out_specs=[ospec, ospec],
        out_shape=[out, out],
        compiler_params=_params(1),
        name="kv_head_rows",
    )(qkv, qkv)


def _suffix_weights():
    r = lax.broadcasted_iota(jnp.int32, (QB, QB), 0)
    c = lax.broadcasted_iota(jnp.int32, (QB, QB), 1)
    w = jnp.concatenate([(r > c).astype(BF16), jnp.ones((QB, QB), BF16)], axis=1)
    return jnp.concatenate([w, w], axis=0)


def _sb_logits(z):
    ls = -(jnp.maximum(z, 0.0) + jnp.log1p(jnp.exp(-jnp.abs(z))))
    return ls, ls + z


def _sb_scores(qbs, kbs, mask, biases, w2):
    lss, lsgs = [], []
    for qb, kb, bias in zip(qbs, kbs, biases):
        z = lax.dot_general(qb, kb, NT_DIMS, preferred_element_type=F32) * (SB_DH ** -0.5) + bias
        ls, lsg = _sb_logits(z)
        lsgs.append(lsg)
        lss.append(ls if mask is None else jnp.where(mask, ls, 0.0))
    return lsgs, _split_dot(jnp.concatenate(lss, axis=0), w2)


def _sb_weigh(lsgs, res, vbs, mask, cbcs, accs):
    new_cbcs, new_accs = [], []
    for i in range(len(vbs)):
        ri = res[QB * i:QB * (i + 1)]
        w = jnp.exp(lsgs[i] + ri[:, :QB] + cbcs[i])
        if mask is not None:
            w = jnp.where(mask, w, 0.0)
        new_accs.append(accs[i] + jnp.dot(w.astype(BF16), vbs[i], preferred_element_type=F32))
        new_cbcs.append(cbcs[i] + ri[:, QB:])
    return tuple(new_cbcs), tuple(new_accs)


def _sb_tile(qbs, kbs, vbs, mask, biases, w2, cbcs, accs):
    lsgs, res = _sb_scores(qbs, kbs, mask, biases, w2)
    return _sb_weigh(lsgs, res, vbs, mask, cbcs, accs)


def _sbp_kernel(bias_ref, q_ref, k_ref, v_ref, o_ref, kpad, vpad, *, n_blk, nh):
    h0 = pl.program_id(1) * nh
    biases = [bias_ref[h0 + i] for i in range(nh)]
    hs = [slice(SB_DH * i, SB_DH * (i + 1)) for i in range(nh)]
    w2 = _suffix_weights()
    r = lax.broadcasted_iota(jnp.int32, (QB, QB), 0)
    c = lax.broadcasted_iota(jnp.int32, (QB, QB), 1)
    diag_mask = c < r
    meta_mask = c < N_META
    zeros = tuple(jnp.zeros((QB, QB), F32) for _ in range(nh))

    kpad[...] = jnp.zeros_like(kpad)
    vpad[...] = jnp.zeros_like(vpad)
    kpad[0:N_META, :] = k_ref[0:N_META, :].astype(BF16)
    vpad[0:N_META, :] = v_ref[0:N_META, :].astype(BF16)
    rows = lambda ref, r0: [ref[pl.ds(r0, QB), s].astype(BF16) for s in hs]
    kms = [kpad[:, s] for s in hs]
    vms = [vpad[:, s] for s in hs]

    _, accs = _sb_tile(rows(q_ref, 0), kms, vms, diag_mask & meta_mask, biases, w2, zeros, zeros)
    for i in range(nh):
        o_ref[0:N_META, hs[i]] = accs[i][0:N_META]

    def q_block(i, carry):
        q0 = pl.multiple_of(N_META + QB * i, 8)
        qbs = rows(q_ref, q0)
        ca = _sb_tile(qbs, rows(k_ref, q0), rows(v_ref, q0), diag_mask, biases, w2, zeros, zeros)

        def k_pair(j, ca):
            m0 = i - 1 - 2 * j
            m1 = m0 - 1
            k0 = pl.multiple_of(N_META + QB * m0, 8)
            k1 = pl.multiple_of(N_META + QB * jnp.maximum(m1, 0), 8)
            real = jnp.full((QB, QB), m1, jnp.int32) >= 0
            s0 = _sb_scores(qbs, rows(k_ref, k0), None, biases, w2)
            s1 = _sb_scores(qbs, rows(k_ref, k1), real, biases, w2)
            ca = _sb_weigh(*s0, rows(v_ref, k0), None, *ca)
            return _sb_weigh(*s1, rows(v_ref, k1), real, *ca)

        ca = lax.fori_loop(0, (i + 1) // 2, k_pair, ca)
        _, accs = _sb_tile(qbs, kms, vms, meta_mask, biases, w2, *ca)
        for hh in range(nh):
            o_ref[pl.ds(q0, QB), hs[hh]] = accs[hh]
        return carry

    lax.fori_loop(0, n_blk, q_block, 0)


def sb_prompt_attn(qkv, bias, n_b, n_t, nh):
    d = qkv.shape[1] // 3
    hb = d // (SB_DH * nh)
    blk = lambda off: pl.BlockSpec((n_t, SB_DH * nh), lambda b, h, off=off: (b, off + h))
    return pl.pallas_call(
        functools.partial(_sbp_kernel, n_blk=(n_t - N_META) // QB, nh=nh),
        grid=(n_b, hb),
        in_specs=[pl.BlockSpec(memory_space=pltpu.SMEM), blk(0), blk(hb), blk(2 * hb)],
        out_specs=pl.BlockSpec((n_t, SB_DH * nh), lambda b, h: (b, h)),
        out_shape=jax.ShapeDtypeStruct((n_b * n_t, d), F32),
        scratch_shapes=[pltpu.VMEM((QB, SB_DH * nh), BF16), pltpu.VMEM((QB, SB_DH * nh), BF16)],
        compiler_params=_params(2),
        name="sb_prompt_attn",
    )(bias, qkv, qkv, qkv)


def _sbs_kernel(pt_ref, bias_ref, q_ref, kn_ref, vn_ref, *rest, n_heads, n_new, pp):
    kp_refs, vp_refs = rest[:pp], rest[pp:2 * pp]
    o_ref, cbc_scr, acc_scr, pad_scr = rest[2 * pp:]
    j = pl.program_id(1)
    w2 = _suffix_weights()
    hs = [slice(SB_DH * h, SB_DH * (h + 1)) for h in range(n_heads)]
    r = lax.broadcasted_iota(jnp.int32, (n_new * n_heads, PAGE), 0)
    c = lax.broadcasted_iota(jnp.int32, (n_new * n_heads, PAGE), 1)
    bias_col = jnp.concatenate([jnp.full((n_new, 1), bias_ref[h], F32) for h in range(n_heads)], axis=0)
    q = q_ref[0]
    qh = [q[n_new * h:n_new * (h + 1)].astype(BF16) for h in range(n_heads)]
    head_rows = lambda ref, h: ref[0, 0, pl.ds(h, PAGE, stride=n_heads), :].astype(BF16)

    def scores(k_at, mask):
        z = jnp.concatenate([lax.dot_general(qh[h], k_at(h), NT_DIMS, preferred_element_type=F32)
                             for h in range(n_heads)], axis=0) * (SB_DH ** -0.5) + bias_col
        ls, lsg = _sb_logits(z)
        if mask is not None:
            ls = jnp.where(mask, ls, 0.0)
        return lsg, _split_dot(ls, w2)

    def weigh(lsg, res, v_at, mask, cbc, acc):
        w = jnp.exp(lsg + res[:, :PAGE] + cbc)
        if mask is not None:
            w = jnp.where(mask, w, 0.0)
        pv = jnp.concatenate(
            [jnp.dot(w[n_new * h:n_new * (h + 1)].astype(BF16), v_at(h), preferred_element_type=F32)
             for h in range(n_heads)], axis=0)
        return cbc + res[:, PAGE:], acc + pv

    @pl.when(j == 0)
    def _():
        pad_scr[...] = jnp.zeros_like(pad_scr)
        pad_scr[0, 0:n_new, :] = kn_ref[0].astype(BF16)
        pad_scr[1, 0:n_new, :] = vn_ref[0].astype(BF16)
        mask = c < (r % n_new)
        zeros = jnp.zeros((n_new * n_heads, PAGE), F32)
        lsg, res = scores(lambda h: pad_scr[0, :, hs[h]], mask)
        cbc, acc = weigh(lsg, res, lambda h: pad_scr[1, :, hs[h]], mask, zeros, zeros)
        cbc_scr[...] = cbc
        acc_scr[...] = acc

    sc = [scores(functools.partial(head_rows, kp), None) for kp in kp_refs]
    cbc, acc = cbc_scr[...], acc_scr[...]
    for (lsg, res), vp in zip(sc, vp_refs):
        cbc, acc = weigh(lsg, res, functools.partial(head_rows, vp), None, cbc, acc)
    cbc_scr[...] = cbc
    acc_scr[...] = acc

    @pl.when(j == pl.num_programs(1) - 1)
    def _():
        o_ref[0] = acc


def sb_sample_attn(q_hq, k_new, v_new, cache_k, cache_v, layer, page_table, bias, pp):
    n_s, n_pages = page_table.shape
    n_new, d = k_new.shape[1:]
    n_heads = d // SB_DH
    assert n_pages % pp == 0
    page_specs = [pl.BlockSpec((1, 1, PAGE * n_heads, SB_DH),
                               lambda s, j, pt, b, i=i: (layer, pt[s, n_pages - 1 - (j * pp + i)], 0, 0))
                  for i in range(pp)]
    seq3 = lambda s, j, pt, b: (s, 0, 0)
    return pl.pallas_call(
        functools.partial(_sbs_kernel, n_heads=n_heads, n_new=n_new, pp=pp),
        grid_spec=pltpu.PrefetchScalarGridSpec(
            num_scalar_prefetch=2,
            grid=(n_s, n_pages // pp),
            in_specs=[pl.BlockSpec((1, n_new * n_heads, SB_DH), seq3),
                      pl.BlockSpec((1, n_new, d), seq3),
                      pl.BlockSpec((1, n_new, d), seq3)] + page_specs + page_specs,
            out_specs=pl.BlockSpec((1, n_new * n_heads, SB_DH), seq3),
            scratch_shapes=[pltpu.VMEM((n_new * n_heads, PAGE), F32), pltpu.VMEM((n_new * n_heads, SB_DH), F32),
                            pltpu.VMEM((2, PAGE, d), BF16)]),
        out_shape=jax.ShapeDtypeStruct((n_s, n_new * n_heads, SB_DH), F32),
        compiler_params=_params(2),
        name="sb_sample_attn",
    )(page_table, bias, q_hq, k_new, v_new, *([cache_k] * pp), *([cache_v] * pp))


def _rw_prep_kernel(k_ref, wl_ref, al_ref, w0_ref, a0_ref, kk_ref_, ka_ref, dec_o, kk_o, b_o, km_o):
    k = k_ref[...]
    w2 = _head_sum_weights(RW_DH)
    wpre = w0_ref[...] + wl_ref[...]
    sp = jnp.maximum(-wpre, 0.0) + jnp.log1p(jnp.exp(-jnp.abs(wpre)))
    dec_o[...] = jnp.exp(-jnp.exp(-sp - 0.5))
    a = jax.nn.sigmoid(a0_ref[...] + al_ref[...])
    kk = k * kk_ref_[...]
    kk = kk / jnp.maximum(jnp.sqrt(_head_sum(kk * kk, w2)), 1e-12)
    kk_o[...] = kk
    b_o[...] = kk * a
    km_o[...] = k * (1.0 + (a - 1.0) * ka_ref[...])


def rw_prep(k, wl, al, w0, a0, k_k, k_a):
    m, d = k.shape
    tm = min(256, m)
    tok = pl.BlockSpec((tm, d), lambda i: (i, 0))
    row = pl.BlockSpec((1, d), lambda i: (0, 0))
    out = jax.ShapeDtypeStruct((m, d), F32)
    return pl.pallas_call(
        _rw_prep_kernel,
        grid=(pl.cdiv(m, tm),),
        in_specs=[tok] * 3 + [row] * 4,
        out_specs=[tok] * 4,
        out_shape=[out] * 4,
        compiler_params=_params(1),
        name="rwkv_prep",
    )(k, wl, al, *(p.reshape(1, d) for p in (w0, a0, k_k, k_a)))


def _wkv_kernel(r_ref, d_ref, k_ref, v_ref, kk_ref, b_ref, s0_ref, y_ref, sT_ref, s_scr, *, n_pairs, tc):
    c = pl.program_id(1)

    @pl.when(c == 0)
    def _():
        for p in range(n_pairs):
            s_scr[RW_DH * p:RW_DH * (p + 1), :] = jnp.concatenate([s0_ref[0, 2 * p], s0_ref[0, 2 * p + 1]], axis=-1)

    row = lax.broadcasted_iota(jnp.int32, (RW_DH, 2 * RW_DH), 0)
    col = lax.broadcasted_iota(jnp.int32, (RW_DH, 2 * RW_DH), 1)
    sel = jnp.tile(row == (col % RW_DH), (n_pairs, 1))
    w2 = _head_sum_weights(RW_DH)

    def group(g, carry):
        t0 = pl.multiple_of(g * 8, 8)
        ld = lambda ref: ref[pl.ds(t0, 8), :]
        r8, d8, k8, v8, kk8, b8 = ld(r_ref), ld(d_ref), ld(k_ref), ld(v_ref), ld(kk_ref), ld(b_ref)
        s = s_scr[...]
        vT = [v8[:, 2 * RW_DH * p:2 * RW_DH * (p + 1)].T for p in range(n_pairs)]
        ys = []
        for j in range(8):
            def bc(x8):
                return jnp.concatenate(
                    [jnp.broadcast_to(x8[j:j + 1, 2 * RW_DH * p:2 * RW_DH * (p + 1)], (RW_DH, 2 * RW_DH))
                     for p in range(n_pairs)], axis=0)
            sa = _split_dot(s * bc(kk8), w2)
            vbc = jnp.concatenate(
                [jnp.concatenate([jnp.broadcast_to(vT[p][RW_DH * hh:RW_DH * (hh + 1), j:j + 1], (RW_DH, RW_DH))
                                  for hh in range(2)], axis=1) for p in range(n_pairs)], axis=0)
            s = s * bc(d8) - sa * bc(b8) + vbc * bc(k8)
            ybc = _split_dot(s * bc(r8), w2)
            ys.append(jnp.sum(jnp.where(sel, ybc, 0.0).reshape(n_pairs, RW_DH, 2 * RW_DH), axis=1))
        s_scr[...] = s
        y_ref[pl.ds(t0, 8), :] = jnp.concatenate(
            [jnp.concatenate([ys[j][p:p + 1] for j in range(8)], axis=0) for p in range(n_pairs)], axis=1)
        return carry

    lax.fori_loop(0, tc // 8, group, 0)

    @pl.when(c == pl.num_programs(1) - 1)
    def _():
        for p in range(n_pairs):
            s = s_scr[RW_DH * p:RW_DH * (p + 1), :]
            sT_ref[0, 2 * p] = s[:, :RW_DH]
            sT_ref[0, 2 * p + 1] = s[:, RW_DH:]


def wkv_scan(r, dec, k, v, kk, b, s0, row0, n_t):
    d = r.shape[1]
    n_b, n_h = s0.shape[:2]
    tc = _divisor_tile(n_t, 48, 8)
    assert row0 % tc == 0
    nc = n_t // tc
    tok = pl.BlockSpec((tc, d), lambda i, c: (row0 // tc + i * nc + c, 0))
    st = pl.BlockSpec((1, n_h, RW_DH, RW_DH), lambda i, c: (i, 0, 0, 0))
    return pl.pallas_call(
        functools.partial(_wkv_kernel, n_pairs=n_h // 2, tc=tc),
        grid=(n_b, nc),
        in_specs=[tok] * 6 + [st],
        out_specs=[pl.BlockSpec((tc, d), lambda i, c: (i * nc + c, 0)), st],
        out_shape=[jax.ShapeDtypeStruct((n_b * n_t, d), F32), jax.ShapeDtypeStruct(s0.shape, F32)],
        scratch_shapes=[pltpu.VMEM((n_h // 2 * RW_DH, 2 * RW_DH), F32)],
        compiler_params=_params(2),
        name="wkv7_scan",
    )(r, dec, k, v, kk, b, s0)


def _rw_post_kernel(y_ref, r_ref, km_ref, v_ref, g_ref, rk_ref, lw_ref, lb_ref, o_ref):
    w2 = _head_sum_weights(RW_DH)
    y = y_ref[...]
    mean = _head_sum(y, w2) * (1.0 / RW_DH)
    yc = y - mean
    var = _head_sum(yc * yc, w2) * (1.0 / RW_DH)
    yn = yc * lax.rsqrt(var + GN_EPS) * lw_ref[...] + lb_ref[...]
    bonus = _head_sum(r_ref[...] * km_ref[...] * rk_ref[...], w2) * v_ref[...]
    o_ref[...] = (yn + bonus) * g_ref[...]


def rw_post(y, r, km, v, g, r_k, ln_w, ln_b):
    m, d = y.shape
    tm = min(256, m)
    tok = pl.BlockSpec((tm, d), lambda i: (i, 0))
    row = pl.BlockSpec((1, d), lambda i: (0, 0))
    return pl.pallas_call(
        _rw_post_kernel,
        grid=(pl.cdiv(m, tm),),
        in_specs=[tok] * 5 + [row] * 3,
        out_specs=tok,
        out_shape=jax.ShapeDtypeStruct((m, d), F32),
        compiler_params=_params(1),
        name="rwkv_post",
    )(y, r, km, v, g, *(p.reshape(1, d) for p in (r_k, ln_w, ln_b)))


def _moe_up_kernel(be_ref, nv_ref, x_ref, wg_ref, wu_ref, o_ref, wg_s, wu_s):
    i = pl.program_id(1)
    e = be_ref[i]
    prev = be_ref[jnp.maximum(i - 1, 0)]

    @pl.when((i == 0) | (e != prev))
    def _():
        wg_s[...] = wg_ref[0, 0].astype(BF16)
        wu_s[...] = wu_ref[0, 0].astype(BF16)

    @pl.when(i < nv_ref[0])
    def _():
        x = x_ref[...].astype(BF16)
        gate = jnp.dot(x, wg_s[...], preferred_element_type=F32)
        up = jnp.dot(x, wu_s[...], preferred_element_type=F32)
        o_ref[...] = gate * jax.nn.sigmoid(gate) * up

    @pl.when(i >= nv_ref[0])
    def _():
        o_ref[...] = jnp.zeros_like(o_ref)


def _moe_down_kernel(be_ref, nv_ref, h_ref, w_ref, o_ref, w_s):
    i = pl.program_id(1)
    e = be_ref[i]
    prev = be_ref[jnp.maximum(i - 1, 0)]

    @pl.when((i == 0) | (e != prev))
    def _():
        w_s[...] = w_ref[0, 0].astype(BF16)

    @pl.when(i < nv_ref[0])
    def _():
        o_ref[...] = jnp.dot(h_ref[...].astype(BF16), w_s[...], preferred_element_type=F32)

    @pl.when(i >= nv_ref[0])
    def _():
        o_ref[...] = jnp.zeros_like(o_ref)


def moe_experts(x_disp, block_e, n_valid, w_up, w_down, layer):
    n_rows, d = x_disp.shape
    n_blocks = n_rows // MOE_BLOCK
    f = w_down.shape[2]
    th = min(COL_TILE, f)
    nf = f // th
    hid = pl.pallas_call(
        _moe_up_kernel,
        grid_spec=pltpu.PrefetchScalarGridSpec(
            num_scalar_prefetch=2,
            grid=(nf, n_blocks),
            in_specs=[pl.BlockSpec((MOE_BLOCK, d), lambda j, i, be, nv: (i, 0)),
                      pl.BlockSpec((1, 1, d, th), lambda j, i, be, nv: (layer, be[i], 0, j)),
                      pl.BlockSpec((1, 1, d, th), lambda j, i, be, nv: (layer, be[i], 0, nf + j))],
            out_specs=pl.BlockSpec((MOE_BLOCK, th), lambda j, i, be, nv: (i, j)),
            scratch_shapes=[pltpu.VMEM((d, th), BF16), pltpu.VMEM((d, th), BF16)]),
        out_shape=jax.ShapeDtypeStruct((n_rows, f), F32),
        compiler_params=_params(2),
        name="moe_up_gate",
    )(block_e, n_valid, x_disp, w_up, w_up)
    tn = min(2 * COL_TILE, d)
    return pl.pallas_call(
        _moe_down_kernel,
        grid_spec=pltpu.PrefetchScalarGridSpec(
            num_scalar_prefetch=2,
            grid=(d // tn, n_blocks),
            in_specs=[pl.BlockSpec((MOE_BLOCK, f), lambda j, i, be, nv: (i, 0)),
                      pl.BlockSpec((1, 1, f, tn), lambda j, i, be, nv: (layer, be[i], 0, j))],
            out_specs=pl.BlockSpec((MOE_BLOCK, tn), lambda j, i, be, nv: (i, j)),
            scratch_shapes=[pltpu.VMEM((f, tn), BF16)]),
        out_shape=jax.ShapeDtypeStruct((n_rows, d), F32),
        compiler_params=_params(2),
        name="moe_down",
    )(block_e, n_valid, hid, w_down)


def hier_moe(h, tok, layer, w_group, b_group, w_expert, b_expert, w_up, w_down):
    n_tok, d = tok.shape
    n_groups = w_group.shape[-1]
    n_experts = w_expert.shape[-1]
    epg = n_experts // n_groups
    logits = matmul(tok, jnp.concatenate([w_group[layer], w_expert[layer]], axis=1))
    g_logit = logits[:, :n_groups] + b_group[layer]
    g_prob = jax.nn.softmax(g_logit, axis=-1)
    g_idx = jnp.argmax(g_logit, axis=-1)
    g_w = jnp.take_along_axis(g_prob, g_idx[:, None], axis=1)
    e_logit = (logits[:, n_groups:] + b_expert[layer]).reshape(n_tok, n_groups, epg)
    e_logit = jnp.take_along_axis(e_logit, g_idx[:, None, None], axis=1)[:, 0]
    top_p, top_i = lax.top_k(jax.nn.softmax(e_logit, axis=-1), TOP_K)
    gates = g_w * top_p / jnp.sum(top_p, axis=-1, keepdims=True)
    flat_e = (g_idx[:, None] * epg + top_i).reshape(-1).astype(jnp.int32)
    n_assign = n_tok * TOP_K
    order = jnp.argsort(flat_e)
    rank = jnp.argsort(order).astype(jnp.int32)
    counts = jnp.sum(flat_e[:, None] == jnp.arange(n_experts, dtype=jnp.int32)[None, :], axis=0, dtype=jnp.int32)
    starts = jnp.cumsum(counts) - counts
    padded = (counts + MOE_BLOCK - 1) // MOE_BLOCK * MOE_BLOCK
    pad_end = jnp.cumsum(padded)
    pad_start = pad_end - padded
    dest = (pad_start[flat_e] + rank - starts[flat_e]).astype(jnp.int32).reshape(n_tok, TOP_K)
    n_blocks = -(-n_assign // MOE_BLOCK) + n_experts
    block_start = jnp.arange(n_blocks, dtype=jnp.int32) * MOE_BLOCK
    block_e = jnp.minimum(jnp.sum(pad_end[None, :] <= block_start[:, None], axis=1), n_experts - 1).astype(jnp.int32)
    row = jnp.arange(n_blocks * MOE_BLOCK, dtype=jnp.int32)
    row_e = jnp.repeat(block_e, MOE_BLOCK)
    src = order[jnp.clip(starts[row_e] + row - pad_start[row_e], 0, n_assign - 1)]
    x_disp = tok[src // TOP_K]
    n_valid = (pad_end[-1:] // MOE_BLOCK).astype(jnp.int32)
    y_disp = moe_experts(x_disp, block_e, n_valid, w_up, w_down, layer)
    for kth in range(TOP_K):
        h = h + gates[:, kth:kth + 1] * y_disp[dest[:, kth]]
    return h


def kernel(x_prompt, x_sample, cache_k, cache_v, state_wkv, state_shift, page_table, meta_tokens, ln_mix, ln_ffn, sb_w_qkv, sb_q_norm, sb_k_norm, sb_logit_bias, sb_w_o, rw_mu, rw_w_r, rw_w_k, rw_w_v, rw_w_o, rw_w0, rw_w_la, rw_w_lb, rw_a0, rw_a_la, rw_a_lb, rw_g_la, rw_g_lb, rw_k_k, rw_k_a, rw_r_k, rw_ln_w, rw_ln_b, moe_w_group, moe_b_group, moe_w_expert, moe_b_expert, moe_w_up, moe_w_down):
    n_b, seq, d = x_prompt.shape
    n_s, n_new, _ = x_sample.shape
    n_t = N_META + seq
    n_p = n_b * n_t
    sb_heads = d // SB_DH
    rw_heads = d // RW_DH
    depth = ln_mix.shape[0]
    moe = (moe_w_group, moe_b_group, moe_w_expert, moe_b_expert, moe_w_up, moe_w_down)

    meta = jnp.broadcast_to(meta_tokens[None], (n_b, N_META, d))
    h = jnp.concatenate([jnp.concatenate([meta, x_prompt], axis=1).reshape(n_p, d), x_sample.reshape(n_s * n_new, d)],
                        axis=0)
    pool = cache_k.shape[1]
    cache_k2 = cache_k.reshape(cache_k.shape[0], pool, PAGE * sb_heads, SB_DH)
    cache_v2 = cache_v.reshape(cache_v.shape[0], pool, PAGE * sb_heads, SB_DH)

    k_p, v_p, k_s, v_s, wkv_p, wkv_s, sh_p, sh_s = [], [], [], [], [], [], [], []
    for i in range(depth):
        li = i // 2
        xn = rmsnorm(h, ln_mix[i])
        if i % 2 == 0:
            qkv = matmul_qkv(xn, sb_w_qkv[li], sb_q_norm[li], sb_k_norm[li])
            kx, vx = qkv[n_p:, d:2 * d], qkv[n_p:, 2 * d:]
            k3, v3 = kv_head_rows(qkv, 0, n_p)
            k_p.append(k3.reshape(n_b, n_t, sb_heads, SB_DH))
            v_p.append(v3.reshape(n_b, n_t, sb_heads, SB_DH))
            k3, v3 = kv_head_rows(qkv, n_p, n_s * n_new)
            k_s.append(k3.reshape(n_s, n_new, sb_heads, SB_DH))
            v_s.append(v3.reshape(n_s, n_new, sb_heads, SB_DH))
            o_p = sb_prompt_attn(qkv, sb_logit_bias[li], n_b, n_t, nh=min(4, sb_heads))
            q_hq = qkv[n_p:, :d].reshape(n_s, n_new, sb_heads, SB_DH).transpose(0, 2, 1, 3)
            o_s = sb_sample_attn(q_hq.reshape(n_s, sb_heads * n_new, SB_DH), kx.reshape(n_s, n_new, d),
                                 vx.reshape(n_s, n_new, d), cache_k2, cache_v2, li, page_table,
                                 sb_logit_bias[li], pp=math.gcd(page_table.shape[1], 8))
            o_s = o_s.reshape(n_s, sb_heads, n_new, SB_DH).transpose(0, 2, 1, 3).reshape(n_s * n_new, d)
            h = matmul_residual(jnp.concatenate([o_p, o_s], axis=0), sb_w_o[li], h)
        else:
            xp3 = xn[:n_p].reshape(n_b, n_t, d)
            xs3 = xn[n_p:].reshape(n_s, n_new, d)
            sh_p.append(xp3[:, -1])
            sh_s.append(xs3[:, -1])
            x_prev = jnp.concatenate(
                [jnp.concatenate([jnp.zeros((n_b, 1, d), F32), xp3[:, :-1]], axis=1).reshape(n_p, d),
                 jnp.concatenate([state_shift[li][:, None], xs3[:, :-1]], axis=1).reshape(n_s * n_new, d)], axis=0)
            mu = rw_mu[li]
            r = matmul_mix(xn, x_prev, mu[0], rw_w_r[li])
            k = matmul_mix(xn, x_prev, mu[2], rw_w_k[li])
            v = matmul_mix(xn, x_prev, mu[3], rw_w_v[li])
            wl1, al1, g1 = lora_in(xn, x_prev, jnp.stack([mu[1], mu[4], mu[5]]), rw_w_la[li], rw_a_la[li],
                                   rw_g_la[li])
            wl = matmul(wl1, rw_w_lb[li])
            al = matmul(al1, rw_a_lb[li])
            g = matmul(g1, rw_g_lb[li])
            dec, kk, b, km = rw_prep(k, wl, al, rw_w0[li], rw_a0[li], rw_k_k[li], rw_k_a[li])
            y_p, s_p = wkv_scan(r, dec, km, v, kk, b, jnp.zeros((n_b, rw_heads, RW_DH, RW_DH), F32), 0, n_t)
            y_s, s_s = wkv_scan(r, dec, km, v, kk, b, state_wkv[li], n_p, n_new)
            wkv_p.append(s_p)
            wkv_s.append(s_s)
            z = rw_post(jnp.concatenate([y_p, y_s], axis=0), r, km, v, g, rw_r_k[li].reshape(-1), rw_ln_w[li],
                        rw_ln_b[li])
            h = matmul_residual(z, rw_w_o[li], h)
        h = hier_moe(h, rmsnorm(h, ln_ffn[i]), i, *moe)
    return (h[:n_p].reshape(n_b, n_t, d)[:, N_META:], h[n_p:].reshape(n_s, n_new, d),
            jnp.stack(k_p), jnp.stack(v_p), jnp.stack(k_s), jnp.stack(v_s),
            jnp.stack(wkv_p), jnp.stack(wkv_s), jnp.stack(sh_p), jnp.stack(sh_s))
```
